```python
import math
import jax, jax.numpy as jnp
from jax import lax
import numpy as np

D_MODEL = 1024
BATCH = 2
SEQ = 8192
DEPTH = 2
DEC_BATCH = 32
DEC_SEQ = 8
PAST_LEN = 16384
PAGE_SIZE = 128

D_MIX = D_MODEL
D_ATT = D_MIX // 2
N_HEADS = 8
HEAD_DIM = D_ATT // N_HEADS
DILATED_GROUPS = ((128, 1), (512, 4), (2048, 16))
ATT_WINDOW = max(w for w, _ in DILATED_GROUPS)
ATT_BLOCK = 128
D_SSM = D_MIX // 4
SSM_GROUP = 16
N_SSM_GROUPS = D_SSM // SSM_GROUP
SSM_STATE = 64
DT_MIN = 1e-3
DT_MAX = 1e-1
D_POOL = D_MIX // 4
POOL_WINDOWS = (2, 4, 8, 16)
N_POOL_GROUPS = len(POOL_WINDOWS)
POOL_GROUP = D_POOL // N_POOL_GROUPS
POOL_BUF = max(POOL_WINDOWS) - 1
D_IN = 3 * D_ATT + D_SSM + D_POOL
D_FF = 2816
CONV_WIDTH = 3
CONV_BUF = CONV_WIDTH - 1
EPS = 1e-6
NEG = -1e30

kernel_name = 'hybrid_dilated_s5_pool_decoder_step'


def _rmsnorm(x, g):
    xf = x.astype(jnp.float32)
    return xf * lax.rsqrt(jnp.mean(xf * xf, axis=-1, keepdims=True) + EPS) * g.astype(jnp.float32)


def _dilated_prompt(q, k, v, window, dilation):
    b, s, h, e = q.shape
    n_back = window // dilation
    blk = ATT_BLOCK
    span = dilation * blk
    s_pad = -(-s // span) * span
    nb = s_pad // span
    pad = ((0, 0), (0, s_pad - s), (0, 0), (0, 0))

    def blocks(t):
        return jnp.pad(t, pad).reshape(b, nb, blk, dilation, h, e)

    def with_prev(t):
        prev = jnp.concatenate([jnp.zeros_like(t[:, :1]), t[:, :-1]], axis=1)
        return jnp.concatenate([prev, t], axis=2)

    qb = blocks(q)
    kc = with_prev(blocks(k))
    vc = with_prev(blocks(v))
    sc = jnp.einsum('bnqrhe,bnkrhe->bnrhqk', qb, kc) * (e ** -0.5)
    qi = jnp.arange(blk)[:, None] + blk
    ki = jnp.arange(2 * blk)[None, :]
    rel = qi - ki
    band = (rel >= 0) & (rel <= n_back)
    has_prev = (jnp.arange(nb)[:, None, None] > 0) | (ki[None] >= blk)
    mask = band[None] & has_prev
    sc = jnp.where(mask[None, :, None, None], sc, NEG)
    m = jnp.max(sc, axis=-1, keepdims=True)
    p = jnp.exp(sc - m)
    l = jnp.sum(p, axis=-1)
    o = jnp.einsum('bnrhqk,bnkrhe->bnqrhe', p, vc) / jnp.transpose(l, (0, 1, 4, 2, 3))[..., None]
    lse = jnp.transpose(m[..., 0] + jnp.log(l), (0, 1, 4, 2, 3))
    return o.reshape(b, s_pad, h, e)[:, :s], lse.reshape(b, s_pad, h)[:, :s]


def _dilated_decode(q, k_all, v_all, n_past, window, dilation):
    t = q.shape[1]
    n_back = window // dilation
    idx = n_past + jnp.arange(t)[:, None] - dilation * jnp.arange(n_back + 1)[None, :]
    valid = idx >= 0
    idx = jnp.maximum(idx, 0)
    kg = k_all[:, idx]
    vg = v_all[:, idx]
    sc = jnp.einsum('bthe,btnhe->bhtn', q, kg) * (q.shape[-1] ** -0.5)
    sc = jnp.where(valid[None, None], sc, NEG)
    m = jnp.max(sc, axis=-1, keepdims=True)
    p = jnp.exp(sc - m)
    l = jnp.sum(p, axis=-1)
    o = jnp.einsum('bhtn,btnhe->bthe', p, vg) / jnp.transpose(l, (0, 2, 1))[..., None]
    lse = jnp.transpose(m[..., 0] + jnp.log(l), (0, 2, 1))
    return o, lse


def _merge_by_denominator(outs, lses):
    w = jax.nn.softmax(jnp.stack(lses, 0), axis=0)
    return jnp.einsum('gbth,gbthe->bthe', w, jnp.stack(outs, 0))


def _cplx_affine_op(e1, e2):
    a1r, a1i, b1r, b1i = e1
    a2r, a2i, b2r, b2i = e2
    return (a2r * a1r - a2i * a1i,
            a2r * a1i + a2i * a1r,
            a2r * b1r - a2i * b1i + b2r,
            a2r * b1i + a2i * b1r + b2i)


def _s5(u, h0_re, h0_im, log_dt, a_re, a_im, b_re, b_im, c_re, c_im, d_skip, w_glu, b_glu):
    f32 = jnp.float32
    bsz, t, _ = u.shape
    a_re = a_re.astype(f32)
    a_im = a_im.astype(f32)
    dt = jnp.exp(log_dt.astype(f32))[:, None]
    mag = jnp.exp(dt * a_re)
    ab_re = mag * jnp.cos(dt * a_im)
    ab_im = mag * jnp.sin(dt * a_im)
    den = a_re * a_re + a_im * a_im
    n_re = ab_re - 1.0
    n_im = ab_im
    co_re = (n_re * a_re + n_im * a_im) / den
    co_im = (n_im * a_re - n_re * a_im) / den
    b_re = b_re.astype(f32)
    b_im = b_im.astype(f32)
    bb_re = co_re[..., None] * b_re - co_im[..., None] * b_im
    bb_im = co_re[..., None] * b_im + co_im[..., None] * b_re
    ug = u.reshape(bsz, t, N_SSM_GROUPS, SSM_GROUP)
    bu_re = jnp.einsum('btgc,gpc->btgp', ug, bb_re)
    bu_im = jnp.einsum('btgc,gpc->btgp', ug, bb_im)
    h0_re = h0_re.astype(f32)
    h0_im = h0_im.astype(f32)
    bu_re = bu_re.at[:, 0].add(ab_re * h0_re - ab_im * h0_im)
    bu_im = bu_im.at[:, 0].add(ab_re * h0_im + ab_im * h0_re)
    shape = bu_re.shape
    elems = (jnp.broadcast_to(ab_re, shape), jnp.broadcast_to(ab_im, shape), bu_re, bu_im)
    _, _, h_re, h_im = lax.associative_scan(_cplx_affine_op, elems, axis=1)
    y = (jnp.einsum('btgp,gcp->btgc', h_re, c_re.astype(f32))
         - jnp.einsum('btgp,gcp->btgc', h_im, c_im.astype(f32))).reshape(bsz, t, D_SSM)
    y = y + d_skip.astype(f32) * u
    g = jax.nn.gelu(y)
    out = g * jax.nn.sigmoid(g @ w_glu.astype(f32) + b_glu.astype(f32))
    return out, h_re[:, -1], h_im[:, -1]


def _pool_mix(u, prefix, start_pos, pool_w, pool_scale):
    f32 = jnp.float32
    b, t, _ = u.shape
    uc = jnp.concatenate([prefix.astype(f32), u.astype(f32)], axis=1)
    cs = jnp.concatenate([jnp.zeros((b, 1, D_POOL), f32), jnp.cumsum(uc, axis=1)], axis=1)
    end = cs[:, POOL_BUF + 1:]
    pos = start_pos + jnp.arange(t)
    cur = uc[:, POOL_BUF:]
    outs = []
    for gi, w in enumerate(POOL_WINDOWS):
        sl = slice(gi * POOL_GROUP, (gi + 1) * POOL_GROUP)
        win_sum = end[..., sl] - cs[:, POOL_BUF + 1 - w:POOL_BUF + 1 - w + t, sl]
        cnt = jnp.minimum(pos + 1, w).astype(f32)[None, :, None]
        outs.append(win_sum / cnt - cur[..., sl])
    pooled = jnp.stack(outs, axis=2)
    mixed = jnp.einsum('btgc,gcd->btgd', pooled, pool_w.astype(f32)).reshape(b, t, D_POOL)
    return mixed * pool_scale.astype(f32), uc[:, -POOL_BUF:]


def _causal_dwconv(h, prefix, w, bias):
    t = h.shape[1]
    hc = jnp.concatenate([prefix.astype(h.dtype), h], axis=1)
    w = w.astype(h.dtype)
    y = bias.astype(h.dtype) + hc[:, 0:t] * w[0]
    for j in range(1, CONV_WIDTH):
        y = y + hc[:, j:j + t] * w[j]
    return y, hc[:, -CONV_BUF:]


def _layer(x, lw, k_past, v_past, h0_re, h0_im, pool_prefix, conv_prefix, start_pos, prompt):
    f32 = jnp.float32
    b, t, _ = x.shape
    hn = _rmsnorm(x, lw['norm1_g'])
    z = hn @ lw['w_in'].astype(f32)
    q, k, v, u_ssm, u_pool = jnp.split(z, [D_ATT, 2 * D_ATT, 3 * D_ATT, 3 * D_ATT + D_SSM], axis=-1)
    q = q.reshape(b, t, N_HEADS, HEAD_DIM)
    k = k.reshape(b, t, N_HEADS, HEAD_DIM)
    v = v.reshape(b, t, N_HEADS, HEAD_DIM)
    outs, lses = [], []
    if prompt:
        for (w, d) in DILATED_GROUPS:
            o, l = _dilated_prompt(q, k, v, w, d)
            outs.append(o)
            lses.append(l)
        n_keep = min(ATT_WINDOW, t)
        k_new, v_new = k[:, t - n_keep:], v[:, t - n_keep:]
    else:
        n_past = k_past.shape[1]
        k_all = jnp.concatenate([k_past.astype(f32), k], axis=1)
        v_all = jnp.concatenate([v_past.astype(f32), v], axis=1)
        for (w, d) in DILATED_GROUPS:
            o, l = _dilated_decode(q, k_all, v_all, n_past, w, d)
            outs.append(o)
            lses.append(l)
        k_new, v_new = k, v
    att = _merge_by_denominator(outs, lses).reshape(b, t, D_ATT)
    ssm_out, h_re, h_im = _s5(u_ssm, h0_re, h0_im, lw['ssm_log_dt'], lw['ssm_a_re'], lw['ssm_a_im'],
                              lw['ssm_b_re'], lw['ssm_b_im'], lw['ssm_c_re'], lw['ssm_c_im'],
                              lw['ssm_d'], lw['ssm_w_glu'], lw['ssm_b_glu'])
    pool_out, pool_state = _pool_mix(u_pool, pool_prefix, start_pos, lw['pool_w'], lw['pool_scale'])
    mix = jnp.concatenate([_rmsnorm(att, lw['out_norm_att']),
                           _rmsnorm(ssm_out, lw['out_norm_ssm']),
                           _rmsnorm(pool_out, lw['out_norm_pool'])], axis=-1)
    x = x.astype(f32) + mix @ lw['w_out'].astype(f32)
    hn2 = _rmsnorm(x, lw['norm2_g'])
    up = hn2 @ lw['w_up'].astype(f32)
    up_c, conv_state = _causal_dwconv(up, conv_prefix, lw['conv_w'], lw['conv_b'])
    a, g = jnp.split(up_c, 2, axis=-1)
    x = x + (jax.nn.silu(g) * a) @ lw['w_down'].astype(f32)
    return x, (k_new, v_new, h_re, h_im, pool_state, conv_state)


def _normal(k, shape, scale):
    return scale * jax.random.normal(k, shape, jnp.float32)


def setup_inputs(seed: int = 0) -> dict:
    key = jax.random.key(seed)
    ks = jax.random.split(key, 32)
    att_buf = min(ATT_WINDOW, PAST_LEN)
    x_prompt = _normal(ks[0], (BATCH, SEQ, D_MODEL), 1.0)
    x_sample = _normal(ks[1], (DEC_BATCH, DEC_SEQ, D_MODEL), 1.0)
    cache_k = _normal(ks[2], (DEPTH, DEC_BATCH, att_buf, N_HEADS, HEAD_DIM), 1.0)
    cache_v = _normal(ks[3], (DEPTH, DEC_BATCH, att_buf, N_HEADS, HEAD_DIM), 1.0)
    state_ssm_re = _normal(ks[4], (DEPTH, DEC_BATCH, N_SSM_GROUPS, SSM_STATE), 0.1)
    state_ssm_im = _normal(ks[5], (DEPTH, DEC_BATCH, N_SSM_GROUPS, SSM_STATE), 0.1)
    state_pool = _normal(ks[6], (DEPTH, DEC_BATCH, POOL_BUF, D_POOL), 1.0)
    state_conv = _normal(ks[7], (DEPTH, DEC_BATCH, CONV_BUF, 2 * D_FF), 1.0)
    norm1_g = 1.0 + _normal(ks[8], (DEPTH, D_MODEL), 0.02)
    w_in = _normal(ks[9], (DEPTH, D_MODEL, D_IN), D_MODEL ** -0.5)
    ssm_log_dt = jax.random.uniform(ks[10], (DEPTH, N_SSM_GROUPS), jnp.float32,
                                    math.log(DT_MIN), math.log(DT_MAX))
    ssm_a_re = -0.5 + _normal(ks[11], (DEPTH, N_SSM_GROUPS, SSM_STATE), 0.01)
    ssm_a_im = (math.pi * jnp.arange(SSM_STATE, dtype=jnp.float32)
                + _normal(ks[12], (DEPTH, N_SSM_GROUPS, SSM_STATE), 0.01))
    ssm_b_re = _normal(ks[13], (DEPTH, N_SSM_GROUPS, SSM_STATE, SSM_GROUP), (2 * SSM_GROUP) ** -0.5)
    ssm_b_im = _normal(ks[14], (DEPTH, N_SSM_GROUPS, SSM_STATE, SSM_GROUP), (2 * SSM_GROUP) ** -0.5)
    ssm_c_re = _normal(ks[15], (DEPTH, N_SSM_GROUPS, SSM_GROUP, SSM_STATE), SSM_STATE ** -0.5)
    ssm_c_im = _normal(ks[16], (DEPTH, N_SSM_GROUPS, SSM_GROUP, SSM_STATE), SSM_STATE ** -0.5)
    ssm_d = _normal(ks[17], (DEPTH, D_SSM), 1.0)
    ssm_w_glu = _normal(ks[18], (DEPTH, D_SSM, D_SSM), D_SSM ** -0.5)
    ssm_b_glu = _normal(ks[19], (DEPTH, D_SSM), 0.02)
    pool_w = _normal(ks[20], (DEPTH, N_POOL_GROUPS, POOL_GROUP, POOL_GROUP), POOL_GROUP ** -0.5)
    pool_scale = 1.0 + _normal(ks[21], (DEPTH, D_POOL), 0.1)
    out_norm_att = 1.0 + _normal(ks[22], (DEPTH, D_ATT), 0.02)
    out_norm_ssm = 1.0 + _normal(ks[23], (DEPTH, D_SSM), 0.02)
    out_norm_pool = 1.0 + _normal(ks[24], (DEPTH, D_POOL), 0.02)
    w_out = _normal(ks[25], (DEPTH, D_MIX, D_MODEL), D_MIX ** -0.5)
    norm2_g = 1.0 + _normal(ks[26], (DEPTH, D_MODEL), 0.02)
    w_up = _normal(ks[27], (DEPTH, D_MODEL, 2 * D_FF), D_MODEL ** -0.5)
    conv_w = _normal(ks[28], (DEPTH, CONV_WIDTH, 2 * D_FF), CONV_WIDTH ** -0.5)
    conv_b = _normal(ks[29], (DEPTH, 2 * D_FF), 0.02)
    w_down = _normal(ks[30], (DEPTH, D_FF, D_MODEL), D_FF ** -0.5)
    norm_f_g = 1.0 + _normal(ks[31], (D_MODEL,), 0.02)
    return {'x_prompt': x_prompt, 'x_sample': x_sample, 'cache_k': cache_k, 'cache_v': cache_v,
            'state_ssm_re': state_ssm_re, 'state_ssm_im': state_ssm_im, 'state_pool': state_pool,
            'state_conv': state_conv, 'norm1_g': norm1_g, 'w_in': w_in, 'ssm_log_dt': ssm_log_dt,
            'ssm_a_re': ssm_a_re, 'ssm_a_im': ssm_a_im, 'ssm_b_re': ssm_b_re, 'ssm_b_im': ssm_b_im,
            'ssm_c_re': ssm_c_re, 'ssm_c_im': ssm_c_im, 'ssm_d': ssm_d, 'ssm_w_glu': ssm_w_glu,
            'ssm_b_glu': ssm_b_glu, 'pool_w': pool_w, 'pool_scale': pool_scale,
            'out_norm_att': out_norm_att, 'out_norm_ssm': out_norm_ssm, 'out_norm_pool': out_norm_pool,
            'w_out': w_out, 'norm2_g': norm2_g, 'w_up': w_up, 'conv_w': conv_w, 'conv_b': conv_b,
            'w_down': w_down, 'norm_f_g': norm_f_g}


def reference(x_prompt, x_sample, cache_k, cache_v, state_ssm_re, state_ssm_im, state_pool, state_conv,
              norm1_g, w_in, ssm_log_dt, ssm_a_re, ssm_a_im, ssm_b_re, ssm_b_im, ssm_c_re, ssm_c_im,
              ssm_d, ssm_w_glu, ssm_b_glu, pool_w, pool_scale, out_norm_att, out_norm_ssm,
              out_norm_pool, w_out, norm2_g, w_up, conv_w, conv_b, w_down, norm_f_g):
    f32 = jnp.float32
    xp = x_prompt.astype(f32)
    xs = x_sample.astype(f32)
    bp = xp.shape[0]
    st_p, st_s = [], []
    for i in range(DEPTH):
        lw = {'norm1_g': norm1_g[i], 'w_in': w_in[i], 'ssm_log_dt': ssm_log_dt[i],
              'ssm_a_re': ssm_a_re[i], 'ssm_a_im': ssm_a_im[i], 'ssm_b_re': ssm_b_re[i],
              'ssm_b_im': ssm_b_im[i], 'ssm_c_re': ssm_c_re[i], 'ssm_c_im': ssm_c_im[i],
              'ssm_d': ssm_d[i], 'ssm_w_glu': ssm_w_glu[i], 'ssm_b_glu': ssm_b_glu[i],
              'pool_w': pool_w[i], 'pool_scale': pool_scale[i], 'out_norm_att': out_norm_att[i],
              'out_norm_ssm': out_norm_ssm[i], 'out_norm_pool': out_norm_pool[i], 'w_out': w_out[i],
              'norm2_g': norm2_g[i], 'w_up': w_up[i], 'conv_w': conv_w[i], 'conv_b': conv_b[i],
              'w_down': w_down[i]}
        zero_h = jnp.zeros((bp, N_SSM_GROUPS, SSM_STATE), f32)
        xp, sp = _layer(xp, lw, None, None, zero_h, zero_h,
                        jnp.zeros((bp, POOL_BUF, D_POOL), f32),
                        jnp.zeros((bp, CONV_BUF, 2 * D_FF), f32), 0, True)
        xs, ss = _layer(xs, lw, cache_k[i], cache_v[i], state_ssm_re[i], state_ssm_im[i],
                        state_pool[i], state_conv[i], PAST_LEN, False)
        st_p.append(sp)
        st_s.append(ss)
    y_prompt = _rmsnorm(xp, norm_f_g).astype(x_prompt.dtype)
    y_sample = _rmsnorm(xs, norm_f_g).astype(x_sample.dtype)
    new_k_prompt = jnp.stack([s[0] for s in st_p], 0)
    new_v_prompt = jnp.stack([s[1] for s in st_p], 0)
    new_ssm_re_prompt = jnp.stack([s[2] for s in st_p], 0)
    new_ssm_im_prompt = jnp.stack([s[3] for s in st_p], 0)
    new_pool_prompt = jnp.stack([s[4] for s in st_p], 0)
    new_conv_prompt = jnp.stack([s[5] for s in st_p], 0)
    new_k_sample = jnp.stack([s[0] for s in st_s], 0)
    new_v_sample = jnp.stack([s[1] for s in st_s], 0)
    new_ssm_re_sample = jnp.stack([s[2] for s in st_s], 0)
    new_ssm_im_sample = jnp.stack([s[3] for s in st_s], 0)
    new_pool_sample = jnp.stack([s[4] for s in st_s], 0)
    new_conv_sample = jnp.stack([s[5] for s in st_s], 0)
    return (y_prompt, y_sample, new_k_prompt, new_v_prompt, new_ssm_re_prompt, new_ssm_im_prompt,
            new_pool_prompt, new_conv_prompt, new_k_sample, new_v_sample, new_ssm_re_sample,
            new_ssm_im_sample, new_pool_sample, new_conv_sample)
```

```python
import functools
import math

import jax
import jax.numpy as jnp
import numpy as np
from jax import lax
from jax.experimental import pallas as pl
from jax.experimental.pallas import tpu as pltpu

F32 = jnp.float32
BF16 = jnp.bfloat16

N_HEADS = 8
HEAD_DIM = 64
D_ATT = N_HEADS * HEAD_DIM
DILATED_GROUPS = ((128, 1), (512, 4), (2048, 16))
ATT_BLOCK = 128
SSM_GROUP = 16
SSM_STATE = 64
POOL_WINDOWS = (2, 4, 8, 16)
POOL_HALO = 16
CONV_WIDTH = 3
PAST_LEN = 16384
EPS = 1e-6
NEG = -1e30

SUBLANES = 8
LANES = 128
FF_CHUNK = 256
VMEM_LIMIT = 56 * 1024 * 1024


def _cparams(*sem):
    return pltpu.CompilerParams(dimension_semantics=sem, vmem_limit_bytes=VMEM_LIMIT)


def _rms(x, g):
    return x * lax.rsqrt(jnp.mean(x * x, axis=-1, keepdims=True) + EPS) * g


def _const_spec(shape):
    nd = len(shape)
    return pl.BlockSpec(shape, lambda *_: (0,) * nd, pipeline_mode=pl.Buffered(1))


def _in_proj_kernel(x_ref, g_ref, w_ref, q_ref, k_ref, v_ref, kb_ref, vb_ref, us_ref, up_ref):
    hn = _rms(x_ref[...], g_ref[...]).astype(BF16)
    z = jnp.dot(hn, w_ref[...], preferred_element_type=F32)
    d_ssm = us_ref.shape[-1]
    q_ref[...] = (z[:, :D_ATT] * (HEAD_DIM ** -0.5)).astype(BF16)
    k = z[:, D_ATT:2 * D_ATT]
    v = z[:, 2 * D_ATT:3 * D_ATT]
    k_ref[...] = k
    v_ref[...] = v
    kb_ref[...] = k.astype(BF16)
    vb_ref[...] = v.astype(BF16)
    us_ref[...] = z[:, 3 * D_ATT:3 * D_ATT + d_ssm]
    up_ref[...] = z[:, 3 * D_ATT + d_ssm:]


def _in_proj(x2d, g, w_bf, d_ssm, d_pool, tile):
    n, dm = x2d.shape
    d_in = w_bf.shape[1]
    row = lambda c: pl.BlockSpec((tile, c), lambda i: (i, 0))
    return pl.pallas_call(
        _in_proj_kernel,
        grid=(n // tile,),
        in_specs=[row(dm), _const_spec((1, dm)), _const_spec((dm, d_in))],
        out_specs=[row(D_ATT)] * 5 + [row(d_ssm), row(d_pool)],
        out_shape=[jax.ShapeDtypeStruct((n, D_ATT), BF16),
                   jax.ShapeDtypeStruct((n, D_ATT), F32), jax.ShapeDtypeStruct((n, D_ATT), F32),
                   jax.ShapeDtypeStruct((n, D_ATT), BF16), jax.ShapeDtypeStruct((n, D_ATT), BF16),
                   jax.ShapeDtypeStruct((n, d_ssm), F32), jax.ShapeDtypeStruct((n, d_pool), F32)],
        compiler_params=_cparams("parallel"),
        name="in_proj",
    )(x2d, g, w_bf)


def _attn_prompt_kernel(q_ref, kp_ref, kc_ref, vp_ref, vc_ref, o_ref, lse_ref):
    n = pl.program_id(2)
    blk = ATT_BLOCK
    q = q_ref[0]
    kcat = jnp.concatenate([kp_ref[0], kc_ref[0]], axis=0)
    vcat = jnp.concatenate([vp_ref[0], vc_ref[0]], axis=0)
    row = lax.broadcasted_iota(jnp.int32, (blk, 2 * blk), 0)
    col = lax.broadcasted_iota(jnp.int32, (blk, 2 * blk), 1)
    rel = row + blk - col
    band = jnp.where(rel >= 0, jnp.where(rel <= blk, 1, 0), 0)
    has_prev = jnp.where(col >= blk, 1, jnp.where(n > 0, 1, 0))
    mask = (band * has_prev) > 0
    low_half = lax.broadcasted_iota(jnp.int32, (blk, LANES), 1) < HEAD_DIM
    zero = jnp.zeros((blk, LANES), BF16)
    for pair in range(D_ATT // LANES):
        sl = slice(pair * LANES, (pair + 1) * LANES)
        qp, kp, vp = q[:, sl], kcat[:, sl], vcat[:, sl]
        outs, lses = [], []
        for half in range(2):
            qm = jnp.where(low_half, qp, zero) if half == 0 else jnp.where(low_half, zero, qp)
            s = lax.dot_general(qm, kp, (((1,), (1,)), ((), ())), preferred_element_type=F32)
            s = jnp.where(mask, s, NEG)
            m = jnp.max(s, axis=-1, keepdims=True)
            p = jnp.exp(s - m)
            l = jnp.sum(p, axis=-1, keepdims=True)
            pv = jnp.dot(p.astype(BF16), vp, preferred_element_type=F32)
            outs.append(pv / l)
            lses.append(jnp.broadcast_to(m + jnp.log(l), (blk, LANES)))
        o_ref[0, :, sl] = jnp.where(low_half, outs[0], outs[1])
        lse_ref[0, :, sl] = jnp.where(low_half, lses[0], lses[1])


def _attn_prompt(q, kb, vb, dilation):
    b, s, _ = q.shape
    blk = ATT_BLOCK
    nb = s // (blk * dilation)
    view = lambda t: t.reshape(b, s // dilation, dilation * D_ATT)
    cur = pl.BlockSpec((1, blk, D_ATT), lambda bi, r, n: (bi, n, r))
    prev = pl.BlockSpec((1, blk, D_ATT), lambda bi, r, n: (bi, jnp.maximum(n - 1, 0), r))
    o, lse = pl.pallas_call(
        _attn_prompt_kernel,
        grid=(b, dilation, nb),
        in_specs=[cur, prev, cur, prev, cur],
        out_specs=[cur, cur],
        out_shape=[jax.ShapeDtypeStruct((b, s // dilation, dilation * D_ATT), F32)] * 2,
        compiler_params=_cparams("parallel", "parallel", "arbitrary"),
        name=f"attn_prompt_d{dilation}",
    )(view(q), view(kb), view(kb), view(vb), view(vb))
    return o.reshape(b, s, D_ATT), lse.reshape(b, s, D_ATT)


def _decode_multiplicity(n_past, t_new, n_keys):
    j = np.arange(t_new)[:, None]
    k = np.arange(n_keys)[None, :]
    delta = n_past + j - k
    c = np.zeros((t_new, n_keys), np.float32)
    for w, d in DILATED_GROUPS:
        c += ((delta >= 0) & (delta <= w) & (delta % d == 0) & (k < n_past + t_new)).astype(np.float32)
    return np.tile(c, (N_HEADS, 1))


def _head_block_mask(t_new):
    r = np.arange(N_HEADS * t_new)[:, None] // t_new
    c = np.arange(D_ATT)[None, :] // HEAD_DIM
    return (r == c).astype(np.float32)


def _attn_decode_kernel(q_ref, kc_ref, vc_ref, kn_ref, vn_ref, mult_ref, bd_ref, o_ref, kall, vall):
    t_new = q_ref.shape[1]
    n_past = kc_ref.shape[1]
    pad = kall.shape[0] - n_past - t_new
    kall[0:n_past, :] = kc_ref[0].astype(BF16)
    vall[0:n_past, :] = vc_ref[0].astype(BF16)
    zpad = jnp.zeros((pad, D_ATT), F32)
    kall[n_past:, :] = jnp.concatenate([kn_ref[0], zpad], axis=0).astype(BF16)
    vall[n_past:, :] = jnp.concatenate([vn_ref[0], zpad], axis=0).astype(BF16)
    bd = bd_ref[...]
    q_rows = jnp.concatenate([q_ref[0].astype(F32)] * N_HEADS, axis=0) * bd
    s = lax.dot_general(q_rows.astype(BF16), kall[...], (((1,), (1,)), ((), ())), preferred_element_type=F32)
    mult = mult_ref[...]
    s = jnp.where(mult > 0, s, NEG)
    m = jnp.max(s, axis=-1, keepdims=True)
    p = mult * jnp.exp(s - m)
    l = jnp.sum(p, axis=-1, keepdims=True)
    o = jnp.dot(p.astype(BF16), vall[...], preferred_element_type=F32) / l
    o = o * bd
    acc = o[0:t_new]
    for h in range(1, N_HEADS):
        acc = acc + o[h * t_new:(h + 1) * t_new]
    o_ref[0] = acc


def _attn_decode(q, k_new, v_new, cache_k, cache_v):
    b, t, _ = q.shape
    n_past = cache_k.shape[1]
    n_keys = n_past + LANES
    mult = jnp.asarray(_decode_multiplicity(n_past, t, n_keys))
    bd = jnp.asarray(_head_block_mask(t))
    seq = lambda r: pl.BlockSpec((1, r, D_ATT), lambda i: (i, 0, 0))
    return pl.pallas_call(
        _attn_decode_kernel,
        grid=(b,),
        in_specs=[seq(t), seq(n_past), seq(n_past), seq(t), seq(t),
                  _const_spec((N_HEADS * t, n_keys)), _const_spec((N_HEADS * t, D_ATT))],
        out_specs=seq(t),
        out_shape=jax.ShapeDtypeStruct((b, t, D_ATT), F32),
        scratch_shapes=[pltpu.VMEM((n_keys, D_ATT), BF16), pltpu.VMEM((n_keys, D_ATT), BF16)],
        compiler_params=_cparams("arbitrary"),
        name="attn_decode",
    )(q, cache_k, cache_v, k_new, v_new, mult, bd)


def _ssm_prep_kernel(are_ref, aim_ref, ldt_ref, bre_ref, bim_ref, pwre_ref, pwim_ref, bbre_ref, bbim_ref):
    a_re, a_im = are_ref[...], aim_ref[...]
    dt = jnp.exp(ldt_ref[...])
    steps = (lax.broadcasted_iota(jnp.int32, pwre_ref.shape, 0) + 1).astype(F32)
    mag = jnp.exp(steps * (dt * a_re))
    ang = steps * (dt * a_im)
    pw_re = mag * jnp.cos(ang)
    pw_im = mag * jnp.sin(ang)
    pwre_ref[...] = pw_re
    pwim_ref[...] = pw_im
    ab_re, ab_im = pw_re[0:1], pw_im[0:1]
    den = a_re * a_re + a_im * a_im
    n_re = ab_re - 1.0
    co_re = (n_re * a_re + ab_im * a_im) / den
    co_im = (ab_im * a_re - n_re * a_im) / den
    b_re, b_im = bre_ref[...], bim_ref[...]
    bbre_ref[...] = co_re * b_re - co_im * b_im
    bbim_ref[...] = co_re * b_im + co_im * b_re


def _ssm_prep(log_dt, a_re, a_im, b_re, b_im):
    g, p = a_re.shape
    n = g * p
    flat = lambda t: t.reshape(1, n)
    cmaj = lambda t: jnp.transpose(t, (2, 0, 1)).reshape(SSM_GROUP, n)
    pw_re, pw_im, bb_re, bb_im = pl.pallas_call(
        _ssm_prep_kernel,
        out_shape=[jax.ShapeDtypeStruct((SUBLANES, n), F32)] * 2 + [jax.ShapeDtypeStruct((SSM_GROUP, n), F32)] * 2,
        name="ssm_prep",
    )(flat(a_re), flat(a_im), flat(jnp.repeat(log_dt, p)), cmaj(b_re), cmaj(b_im))
    eye = jnp.eye(g, dtype=F32)

    def expand(bb):
        return (bb.reshape(1, SSM_GROUP, g, p) * eye[:, None, :, None]).reshape(g * SSM_GROUP, n)

    b_big = jnp.concatenate([expand(bb_re), expand(bb_im)], axis=1).astype(BF16)
    return pw_re, pw_im, b_big


def _c_big(c_re, c_im):
    g, c, p = c_re.shape
    eye = jnp.eye(g, dtype=F32)

    def expand(t):
        return (jnp.transpose(t, (0, 2, 1))[:, :, None, :] * eye[:, None, :, None]).reshape(g * p, g * c)

    return jnp.concatenate([expand(c_re), -expand(c_im)], axis=0).astype(BF16)


def _ssm_kernel(u_ref, h0re_ref, h0im_ref, pwre_ref, pwim_ref, bbig_ref, cbig_ref, d_ref, wglu_ref, bglu_ref,
                out_ref, hre_ref, him_ref, bu_scr, h_scr, cre_scr, cim_scr, *, per_slab_state):
    i = pl.program_id(1)
    tt = u_ref.shape[1]
    n = pwre_ref.shape[1]
    u = u_ref[0]
    bu_scr[...] = jnp.dot(u.astype(BF16), bbig_ref[...], preferred_element_type=F32)

    if not per_slab_state:
        @pl.when(i == 0)
        def _():
            cre_scr[...] = h0re_ref[0]
            cim_scr[...] = h0im_ref[0]

    pw_re, pw_im = pwre_ref[...], pwim_ref[...]
    rows = lax.broadcasted_iota(jnp.int32, (SUBLANES, n), 0)
    step_coef = []
    for sh in (1, 2, 4):
        keep = rows >= sh
        step_coef.append((sh, jnp.where(keep, jnp.broadcast_to(pw_re[sh - 1:sh], (SUBLANES, n)), 0.0),
                          jnp.where(keep, jnp.broadcast_to(pw_im[sh - 1:sh], (SUBLANES, n)), 0.0)))

    def slab(j, carry):
        c_re, c_im = carry
        r0 = pl.multiple_of(j * SUBLANES, SUBLANES)
        if per_slab_state:
            c_re = h0re_ref[0, pl.ds(j, 1), :]
            c_im = h0im_ref[0, pl.ds(j, 1), :]
        x_re = bu_scr[pl.ds(r0, SUBLANES), 0:n]
        x_im = bu_scr[pl.ds(r0, SUBLANES), n:2 * n]
        for sh, a_re, a_im in step_coef:
            s_re = pltpu.roll(x_re, sh, 0)
            s_im = pltpu.roll(x_im, sh, 0)
            x_re, x_im = x_re + a_re * s_re - a_im * s_im, x_im + a_re * s_im + a_im * s_re
        x_re, x_im = x_re + pw_re * c_re - pw_im * c_im, x_im + pw_re * c_im + pw_im * c_re
        h_scr[pl.ds(r0, SUBLANES), 0:n] = x_re
        h_scr[pl.ds(r0, SUBLANES), n:2 * n] = x_im
        c_re, c_im = x_re[SUBLANES - 1:SUBLANES], x_im[SUBLANES - 1:SUBLANES]
        if per_slab_state:
            hre_ref[0, pl.ds(j, 1), :] = c_re
            him_ref[0, pl.ds(j, 1), :] = c_im
        return c_re, c_im

    if per_slab_state:
        init = (jnp.zeros((1, n), F32), jnp.zeros((1, n), F32))
    else:
        init = (cre_scr[...], cim_scr[...])
    c_re, c_im = lax.fori_loop(0, tt // SUBLANES, slab, init)
    if not per_slab_state:
        cre_scr[...] = c_re
        cim_scr[...] = c_im
        hre_ref[0] = c_re
        him_ref[0] = c_im

    y = jnp.dot(h_scr[...].astype(BF16), cbig_ref[...], preferred_element_type=F32) + d_ref[...] * u
    gl = jax.nn.gelu(y)
    gate = jnp.dot(gl.astype(BF16), wglu_ref[...], preferred_element_type=F32) + bglu_ref[...]
    out_ref[0] = gl * jax.nn.sigmoid(gate)


def _ssm(u, h0_re, h0_im, pw_re, pw_im, b_big, c_big, d_skip, w_glu_bf, b_glu, tile, per_slab_state):
    b, t, d_ssm = u.shape
    ns, n = h0_re.shape[1], h0_re.shape[2]
    st = pl.BlockSpec((1, ns, n), lambda bi, i: (bi, 0, 0))
    tok = pl.BlockSpec((1, tile, d_ssm), lambda bi, i: (bi, i, 0))
    return pl.pallas_call(
        functools.partial(_ssm_kernel, per_slab_state=per_slab_state),
        grid=(b, t // tile),
        in_specs=[tok, st, st, _const_spec((SUBLANES, n)), _const_spec((SUBLANES, n)),
                  _const_spec((d_ssm, 2 * n)), _const_spec((2 * n, d_ssm)), _const_spec((1, d_ssm)),
                  _const_spec((d_ssm, d_ssm)), _const_spec((1, d_ssm))],
        out_specs=[tok, st, st],
        out_shape=[jax.ShapeDtypeStruct((b, t, d_ssm), F32), jax.ShapeDtypeStruct((b, ns, n), F32),
                   jax.ShapeDtypeStruct((b, ns, n), F32)],
        scratch_shapes=[pltpu.VMEM((tile, 2 * n), F32), pltpu.VMEM((tile, 2 * n), F32),
                        pltpu.VMEM((1, n), F32), pltpu.VMEM((1, n), F32)],
        compiler_params=_cparams("arbitrary", "arbitrary"),
        name="ssm_slab" if per_slab_state else "ssm_seq",
    )(u, h0_re, h0_im, pw_re, pw_im, b_big, c_big, d_skip, w_glu_bf, b_glu)


def _pool_kernel(halo_ref, u_ref, w_ref, scale_ref, o_ref, *, start_pos, zero_first_halo):
    i = pl.program_id(1)
    tt, c = u_ref.shape[1], u_ref.shape[2]
    halo = halo_ref[0]
    if zero_first_halo:
        halo = jnp.where(i > 0, halo, 0.0)
    u = u_ref[0]
    xe = jnp.concatenate([halo, u], axis=0)
    sums, acc, sh = [], xe, 1
    for _ in POOL_WINDOWS:
        acc = acc + pltpu.roll(acc, sh, 0)
        sums.append(acc[POOL_HALO:])
        sh *= 2
    lane = lax.broadcasted_iota(jnp.int32, (tt, c), 1)
    group_width = c // len(POOL_WINDOWS)
    win, wsz = sums[-1], jnp.full((tt, c), POOL_WINDOWS[-1], jnp.int32)
    for gi in range(len(POOL_WINDOWS) - 2, -1, -1):
        in_group = lane < (gi + 1) * group_width
        win = jnp.where(in_group, sums[gi], win)
        wsz = jnp.where(in_group, POOL_WINDOWS[gi], wsz)
    pos = start_pos + i * tt + lax.broadcasted_iota(jnp.int32, (tt, c), 0)
    cnt = jnp.minimum(pos + 1, wsz).astype(F32)
    pooled = win / cnt - u
    o_ref[0] = jnp.dot(pooled.astype(BF16), w_ref[...], preferred_element_type=F32) * scale_ref[...]


def _pool(u, prefix16, w_big_bf, scale, tile, start_pos):
    b, t, c = u.shape
    tok = pl.BlockSpec((1, tile, c), lambda bi, i: (bi, i, 0))
    if prefix16 is None:
        per = tile // POOL_HALO
        halo_src = u
        halo = pl.BlockSpec((1, POOL_HALO, c), lambda bi, i: (bi, jnp.maximum(i * per - 1, 0), 0))
    else:
        assert t == tile
        halo_src = prefix16
        halo = pl.BlockSpec((1, POOL_HALO, c), lambda bi, i: (bi, 0, 0))
    return pl.pallas_call(
        functools.partial(_pool_kernel, start_pos=start_pos, zero_first_halo=prefix16 is None),
        grid=(b, t // tile),
        in_specs=[halo, tok, _const_spec((c, c)), _const_spec((1, c))],
        out_specs=tok,
        out_shape=jax.ShapeDtypeStruct((b, t, c), F32),
        compiler_params=_cparams("parallel", "arbitrary"),
        name="pool_seq" if prefix16 is None else "pool_prefix",
    )(halo_src, u, w_big_bf, scale)


def _ffn(x1, g2_ref, wup_ref, cw_ref, cb_ref, wdown_ref, acc_scr, shifted_rows):
    hn2 = _rms(x1, g2_ref[...]).astype(BF16)
    acc_scr[...] = jnp.zeros_like(acc_scr)
    n_chunks = wup_ref.shape[1]

    def chunk(c, carry):
        act = []
        for part in range(2):
            up = jnp.dot(hn2, wup_ref[part, c], preferred_element_type=F32)
            up1, up2 = shifted_rows(up, part, c)
            cw = cw_ref[part, c]
            act.append(cb_ref[part, c] + up2 * cw[0:1] + up1 * cw[1:2] + up * cw[2:3])
        h = (jax.nn.silu(act[1]) * act[0]).astype(BF16)
        acc_scr[...] += jnp.dot(h, wdown_ref[c], preferred_element_type=F32)
        return carry

    lax.fori_loop(0, n_chunks, chunk, 0)
    return x1 + acc_scr[...]


def _mix_out(att, ssm_ref, pool_ref, x_ref, gatt_ref, gssm_ref, gpool_ref, wout_ref):
    d_ssm = ssm_ref.shape[-1]
    mix = jnp.dot(_rms(att, gatt_ref[...]).astype(BF16), wout_ref[0:D_ATT, :], preferred_element_type=F32)
    mix += jnp.dot(_rms(ssm_ref[0], gssm_ref[...]).astype(BF16), wout_ref[D_ATT:D_ATT + d_ssm, :],
                   preferred_element_type=F32)
    mix += jnp.dot(_rms(pool_ref[0], gpool_ref[...]).astype(BF16), wout_ref[D_ATT + d_ssm:, :],
                   preferred_element_type=F32)
    return x_ref[0] + mix


def _out_seq_kernel(o1_ref, o2_ref, o3_ref, l1_ref, l2_ref, l3_ref, ssm_ref, pool_ref, x_ref,
                    gatt_ref, gssm_ref, gpool_ref, wout_ref, g2_ref, wup_ref, cw_ref, cb_ref, wdown_ref, gf_ref,
                    *rest, final):
    if final:
        xo_ref, y_ref, conv_ref, acc_scr, tail_scr = rest
    else:
        xo_ref, conv_ref, acc_scr, tail_scr = rest
    i = pl.program_id(1)
    tt = x_ref.shape[1]

    @pl.when(i == 0)
    def _():
        tail_scr[...] = jnp.zeros_like(tail_scr)

    l1, l2, l3 = l1_ref[0], l2_ref[0], l3_ref[0]
    lmax = jnp.maximum(jnp.maximum(l1, l2), l3)
    e1, e2, e3 = jnp.exp(l1 - lmax), jnp.exp(l2 - lmax), jnp.exp(l3 - lmax)
    att = (e1 * o1_ref[0] + e2 * o2_ref[0] + e3 * o3_ref[0]) / (e1 + e2 + e3)
    x1 = _mix_out(att, ssm_ref, pool_ref, x_ref, gatt_ref, gssm_ref, gpool_ref, wout_ref)

    def shifted_rows(up, part, c):
        ext = jnp.concatenate([tail_scr[part, c], up], axis=0)
        tail_scr[part, c] = up[tt - SUBLANES:]
        conv_ref[0, part, c] = up[tt - (CONV_WIDTH - 1):]
        return pltpu.roll(ext, 1, 0)[SUBLANES:], pltpu.roll(ext, 2, 0)[SUBLANES:]

    x2 = _ffn(x1, g2_ref, wup_ref, cw_ref, cb_ref, wdown_ref, acc_scr, shifted_rows)
    xo_ref[0] = x2
    if final:
        y_ref[0] = _rms(x2, gf_ref[...])


def _out_slab_kernel(att_ref, ssm_ref, pool_ref, x_ref, p1_ref, p2_ref,
                     gatt_ref, gssm_ref, gpool_ref, wout_ref, g2_ref, wup_ref, cw_ref, cb_ref, wdown_ref, gf_ref,
                     *rest, final):
    if final:
        xo_ref, y_ref, up_ref, acc_scr = rest
    else:
        xo_ref, up_ref, acc_scr = rest
    tt = x_ref.shape[1]
    x1 = _mix_out(att_ref[0], ssm_ref, pool_ref, x_ref, gatt_ref, gssm_ref, gpool_ref, wout_ref)
    t_in = lax.broadcasted_iota(jnp.int32, (tt, FF_CHUNK), 0) & (SUBLANES - 1)

    def shifted_rows(up, part, c):
        up_ref[part, c] = up
        return (jnp.where(t_in >= 1, pltpu.roll(up, 1, 0), p1_ref[part, c]),
                jnp.where(t_in >= 2, pltpu.roll(up, 2, 0), p2_ref[part, c]))

    x2 = _ffn(x1, g2_ref, wup_ref, cw_ref, cb_ref, wdown_ref, acc_scr, shifted_rows)
    xo_ref[0] = x2
    if final:
        y_ref[0] = _rms(x2, gf_ref[...])


def _weight_specs(dm, d_ssm, d_pool, n_chunks):
    return [_const_spec((1, D_ATT)), _const_spec((1, d_ssm)), _const_spec((1, d_pool)),
            _const_spec((dm, dm)), _const_spec((1, dm)),
            _const_spec((2, n_chunks, dm, FF_CHUNK)), _const_spec((2, n_chunks, CONV_WIDTH, FF_CHUNK)),
            _const_spec((2, n_chunks, 1, FF_CHUNK)), _const_spec((n_chunks, FF_CHUNK, dm)), _const_spec((1, dm))]


def _out_seq(att_parts, ssm_out, pool_out, x, lw, tile, final):
    b, t, dm = x.shape
    d_ssm, d_pool = ssm_out.shape[-1], pool_out.shape[-1]
    n_chunks = lw['w_down'].shape[0]
    tok = lambda c: pl.BlockSpec((1, tile, c), lambda bi, i: (bi, i, 0))
    conv_spec = pl.BlockSpec((1, 2, n_chunks, CONV_WIDTH - 1, FF_CHUNK), lambda bi, i: (bi, 0, 0, 0, 0))
    out_specs = [tok(dm)] + ([tok(dm)] if final else []) + [conv_spec]
    out_shape = ([jax.ShapeDtypeStruct((b, t, dm), F32)] * (2 if final else 1)
                 + [jax.ShapeDtypeStruct((b, 2, n_chunks, CONV_WIDTH - 1, FF_CHUNK), F32)])
    (o1, l1), (o2, l2), (o3, l3) = att_parts
    res = pl.pallas_call(
        functools.partial(_out_seq_kernel, final=final),
        grid=(b, t // tile),
        in_specs=[tok(D_ATT)] * 6 + [tok(d_ssm), tok(d_pool), tok(dm)] + _weight_specs(dm, d_ssm, d_pool, n_chunks),
        out_specs=out_specs,
        out_shape=out_shape,
        scratch_shapes=[pltpu.VMEM((tile, dm), F32), pltpu.VMEM((2, n_chunks, SUBLANES, FF_CHUNK), F32)],
        compiler_params=_cparams("parallel", "arbitrary"),
        name="out_seq",
    )(o1, o2, o3, l1, l2, l3, ssm_out, pool_out, x,
      lw['g_att'], lw['g_ssm'], lw['g_pool'], lw['w_out'], lw['g2'], lw['w_up'], lw['conv_w'], lw['conv_b'],
      lw['w_down'], lw['g_f'])
    conv_state = jnp.transpose(res[-1], (0, 3, 1, 2, 4)).reshape(b, CONV_WIDTH - 1, 2 * n_chunks * FF_CHUNK)
    return res[0], (res[1] if final else None), conv_state


def _out_slab(att, ssm_out, pool_out, x, conv_prefix, lw, final):
    _, r, dm = x.shape
    d_ssm, d_pool = ssm_out.shape[-1], pool_out.shape[-1]
    n_chunks = lw['w_down'].shape[0]
    b = conv_prefix.shape[0]
    t = r // b
    assert t == SUBLANES and CONV_WIDTH == 3

    def chunked(rows):
        return jnp.transpose(rows.reshape(r, 2, n_chunks, FF_CHUNK), (1, 2, 0, 3))

    zeros = jnp.zeros((b, 1, conv_prefix.shape[-1]), F32)
    p1 = jnp.concatenate([conv_prefix[:, 1:2]] + [zeros] * (t - 1), axis=1)
    p2 = jnp.concatenate([conv_prefix[:, 0:1], conv_prefix[:, 1:2]] + [zeros] * (t - 2), axis=1)
    p1, p2 = chunked(p1.reshape(r, -1)), chunked(p2.reshape(r, -1))
    full = lambda shape: pl.BlockSpec(shape, lambda: (0,) * len(shape))
    up_shape = (2, n_chunks, r, FF_CHUNK)
    out_specs = [full((1, r, dm))] + ([full((1, r, dm))] if final else []) + [full(up_shape)]
    out_shape = ([jax.ShapeDtypeStruct((1, r, dm), F32)] * (2 if final else 1)
                 + [jax.ShapeDtypeStruct(up_shape, F32)])
    wspecs = [full(s.block_shape) for s in _weight_specs(dm, d_ssm, d_pool, n_chunks)]
    res = pl.pallas_call(
        functools.partial(_out_slab_kernel, final=final),
        in_specs=[full((1, r, D_ATT)), full((1, r, d_ssm)), full((1, r, d_pool)), full((1, r, dm)),
                  full(up_shape), full(up_shape)] + wspecs,
        out_specs=out_specs,
        out_shape=out_shape,
        scratch_shapes=[pltpu.VMEM((r, dm), F32)],
        compiler_params=pltpu.CompilerParams(vmem_limit_bytes=VMEM_LIMIT),
        name="out_slab",
    )(att, ssm_out, pool_out, x, p1, p2,
      lw['g_att'], lw['g_ssm'], lw['g_pool'], lw['w_out'], lw['g2'], lw['w_up'], lw['conv_w'], lw['conv_b'],
      lw['w_down'], lw['g_f'])
    up = jnp.transpose(res[-1], (2, 0, 1, 3)).reshape(b, t, 2 * n_chunks * FF_CHUNK)
    return res[0], (res[1] if final else None), up[:, t - (CONV_WIDTH - 1):]


def _block_diag(w):
    g, a, b = w.shape
    eye = jnp.eye(g, dtype=w.dtype)
    return (w[:, :, None, :] * eye[:, None, :, None]).reshape(g * a, g * b)


def _layer_weights(i, p):
    dm = p['w_in'].shape[1]
    d_ff = p['w_down'].shape[1]
    n_chunks = d_ff // FF_CHUNK
    assert n_chunks * FF_CHUNK == d_ff
    row = lambda t: t.reshape(1, -1)
    chunk_cols = lambda t: jnp.transpose(t.reshape(t.shape[0], 2, n_chunks, FF_CHUNK), (1, 2, 0, 3))
    return {
        'g1': row(p['norm1_g'][i]), 'w_in': p['w_in'][i].astype(BF16),
        'g_att': row(p['out_norm_att'][i]), 'g_ssm': row(p['out_norm_ssm'][i]),
        'g_pool': row(p['out_norm_pool'][i]),
        'w_out': p['w_out'][i].astype(BF16), 'g2': row(p['norm2_g'][i]),
        'w_up': chunk_cols(p['w_up'][i].astype(BF16)),
        'conv_w': chunk_cols(p['conv_w'][i]), 'conv_b': chunk_cols(row(p['conv_b'][i])),
        'w_down': p['w_down'][i].astype(BF16).reshape(n_chunks, FF_CHUNK, dm),
        'g_f': row(p['norm_f_g']),
        'ssm_d': row(p['ssm_d'][i]), 'w_glu': p['ssm_w_glu'][i].astype(BF16), 'b_glu': row(p['ssm_b_glu'][i]),
        'pool_w': _block_diag(p['pool_w'][i]).astype(BF16), 'pool_scale': row(p['pool_scale'][i]),
    }


def kernel(x_prompt, x_sample, cache_k, cache_v, state_ssm_re, state_ssm_im, state_pool, state_conv, norm1_g, w_in, ssm_log_dt, ssm_a_re, ssm_a_im, ssm_b_re, ssm_b_im, ssm_c_re, ssm_c_im, ssm_d, ssm_w_glu, ssm_b_glu, pool_w, pool_scale, out_norm_att, out_norm_ssm, out_norm_pool, w_out, norm2_g, w_up, conv_w, conv_b, w_down, norm_f_g):
    params = dict(norm1_g=norm1_g, w_in=w_in, ssm_d=ssm_d, ssm_w_glu=ssm_w_glu, ssm_b_glu=ssm_b_glu,
                  pool_w=pool_w, pool_scale=pool_scale, out_norm_att=out_norm_att, out_norm_ssm=out_norm_ssm,
                  out_norm_pool=out_norm_pool, w_out=w_out, norm2_g=norm2_g, w_up=w_up, conv_w=conv_w,
                  conv_b=conv_b, w_down=w_down, norm_f_g=norm_f_g)
    depth = w_in.shape[0]
    bp, sp, dm = x_prompt.shape
    bs, ts, _ = x_sample.shape
    n_past = cache_k.shape[2]
    past_len = PAST_LEN
    n_state = ssm_a_re.shape[1] * ssm_a_re.shape[2]
    d_ssm, d_pool = ssm_d.shape[1], pool_scale.shape[1]
    pool_buf = state_pool.shape[2]
    n_keep = min(n_past, sp)
    assert ts == SUBLANES

    xp = x_prompt.astype(F32)
    xs = x_sample.astype(F32).reshape(1, bs * ts, dm)
    yp = ys = None
    st_p, st_s = [], []
    for i in range(depth):
        lw = _layer_weights(i, params)
        final = i == depth - 1
        pw_re, pw_im, b_big = _ssm_prep(ssm_log_dt[i], ssm_a_re[i], ssm_a_im[i], ssm_b_re[i], ssm_b_im[i])
        c_big = _c_big(ssm_c_re[i], ssm_c_im[i])

        q, k, v, kb, vb, u_ssm, u_pool = _in_proj(xp.reshape(bp * sp, dm), lw['g1'], lw['w_in'], d_ssm, d_pool, 512)
        seq = lambda t: t.reshape(bp, sp, t.shape[-1])
        att_parts = [_attn_prompt(seq(q), seq(kb), seq(vb), d) for _, d in DILATED_GROUPS]
        zero_h = jnp.zeros((bp, 1, n_state), F32)
        ssm_out, h_re, h_im = _ssm(seq(u_ssm), zero_h, zero_h, pw_re, pw_im, b_big, c_big, lw['ssm_d'],
                                   lw['w_glu'], lw['b_glu'], 512, False)
        pool_out = _pool(seq(u_pool), None, lw['pool_w'], lw['pool_scale'], 512, 0)
        xp, yp_i, conv_p = _out_seq(att_parts, ssm_out, pool_out, xp, lw, 256, final)
        yp = yp_i if final else yp
        grp = ssm_a_re.shape[1]
        st_p.append((seq(k)[:, sp - n_keep:].reshape(bp, n_keep, N_HEADS, HEAD_DIM),
                     seq(v)[:, sp - n_keep:].reshape(bp, n_keep, N_HEADS, HEAD_DIM),
                     h_re.reshape(bp, grp, -1), h_im.reshape(bp, grp, -1),
                     seq(u_pool)[:, sp - pool_buf:], conv_p))

        q, k, v, _, _, u_ssm, u_pool = _in_proj(xs[0], lw['g1'], lw['w_in'], d_ssm, d_pool, bs * ts)
        per_seq = lambda t: t.reshape(bs, ts, t.shape[-1])
        att = _attn_decode(per_seq(q), per_seq(k), per_seq(v),
                           cache_k[i].reshape(bs, n_past, D_ATT), cache_v[i].reshape(bs, n_past, D_ATT))
        ssm_out, h_re, h_im = _ssm(u_ssm[None], state_ssm_re[i].reshape(1, bs, n_state),
                                   state_ssm_im[i].reshape(1, bs, n_state), pw_re, pw_im, b_big, c_big,
                                   lw['ssm_d'], lw['w_glu'], lw['b_glu'], bs * ts, True)
        prefix16 = jnp.concatenate([jnp.zeros((bs, POOL_HALO - pool_buf, d_pool), F32),
                                    state_pool[i].astype(F32)], axis=1)
        pool_out = _pool(per_seq(u_pool), prefix16, lw['pool_w'], lw['pool_scale'], ts, past_len)
        xs, ys_i, conv_s = _out_slab(att.reshape(1, bs * ts, D_ATT), ssm_out, pool_out.reshape(1, bs * ts, d_pool),
                                     xs, state_conv[i].astype(F32), lw, final)
        ys = ys_i if final else ys
        pool_state = jnp.concatenate([state_pool[i].astype(F32), per_seq(u_pool)], axis=1)[:, -pool_buf:]
        st_s.append((per_seq(k).reshape(bs, ts, N_HEADS, HEAD_DIM), per_seq(v).reshape(bs, ts, N_HEADS, HEAD_DIM),
                     h_re.reshape(bs, grp, -1), h_im.reshape(bs, grp, -1), pool_state, conv_s))

    stack = lambda sts, j: jnp.stack([s[j] for s in sts], 0)
    return ((yp.astype(x_prompt.dtype), ys.reshape(bs, ts, dm).astype(x_sample.dtype))
            + tuple(stack(st_p, j) for j in range(6)) + tuple(stack(st_s, j) for j in range(6)))
```

```python
import functools

import jax
import jax.numpy as jnp
import numpy as np
from jax import lax
from jax.experimental import pallas as pl
from jax.experimental.pallas import tpu as pltpu

F32 = jnp.float32
BF16 = jnp.bfloat16

N_HEADS = 8
HEAD_DIM = 64
D_ATT = N_HEADS * HEAD_DIM
DILATED_GROUPS = ((128, 1), (512, 4), (2048, 16))
ATT_BLOCK = 128
ATT_SPAN = ATT_BLOCK * max(d for _, d in DILATED_GROUPS)
SSM_GROUP = 16
SSM_STATE = 64
POOL_WINDOWS = (2, 4, 8, 16)
POOL_HALO = 16
CONV_WIDTH = 3
PAST_LEN = 16384
EPS = 1e-6
NEG = -1e30

SUBLANES = 8
LANES = 128
HEAD_PAIRS = D_ATT // LANES
FF_CHUNK = 256
VMEM_LIMIT = 56 * 1024 * 1024


def _cparams(*sem):
    return pltpu.CompilerParams(dimension_semantics=sem, vmem_limit_bytes=VMEM_LIMIT)


def _rms(x, g):
    return x * lax.rsqrt(jnp.mean(x * x, axis=-1, keepdims=True) + EPS) * g


def _const_spec(shape):
    nd = len(shape)
    return pl.BlockSpec(shape, lambda *_: (0,) * nd, pipeline_mode=pl.Buffered(1))


def _in_proj_kernel(x_ref, g_ref, w_ref, q_ref, k_ref, v_ref, us_ref, up_ref):
    hn = _rms(x_ref[...], g_ref[...]).astype(BF16)
    z = jnp.dot(hn, w_ref[...], preferred_element_type=F32)
    d_ssm = us_ref.shape[-1]
    for p in range(HEAD_PAIRS):
        lo = p * LANES
        q_ref[p] = z[:, lo:lo + LANES] * (HEAD_DIM ** -0.5)
        k_ref[p] = z[:, D_ATT + lo:D_ATT + lo + LANES]
        v_ref[p] = z[:, 2 * D_ATT + lo:2 * D_ATT + lo + LANES]
    us_ref[...] = z[:, 3 * D_ATT:3 * D_ATT + d_ssm]
    up_ref[...] = z[:, 3 * D_ATT + d_ssm:]


def _in_proj(x2d, g, w_bf, d_ssm, d_pool, tile):
    n, dm = x2d.shape
    d_in = w_bf.shape[1]
    row = lambda c: pl.BlockSpec((tile, c), lambda i: (i, 0))
    pairs = pl.BlockSpec((HEAD_PAIRS, tile, LANES), lambda i: (0, i, 0))
    return pl.pallas_call(
        _in_proj_kernel,
        grid=(n // tile,),
        in_specs=[row(dm), _const_spec((1, dm)), _const_spec((dm, d_in))],
        out_specs=[pairs] * 3 + [row(d_ssm), row(d_pool)],
        out_shape=[jax.ShapeDtypeStruct((HEAD_PAIRS, n, LANES), F32)] * 3
                  + [jax.ShapeDtypeStruct((n, d_ssm), F32), jax.ShapeDtypeStruct((n, d_pool), F32)],
        compiler_params=_cparams("parallel"),
        name="in_proj",
    )(x2d, g, w_bf)


def _pairs_to_heads(t):
    t = jnp.moveaxis(t, 0, -2)
    return t.reshape(t.shape[:-2] + (N_HEADS, HEAD_DIM))


def _softmax_pv(qm, k, v, mask):
    s = lax.dot_general(qm, k, (((1,), (1,)), ((), ())), preferred_element_type=F32)
    s = jnp.where(mask, s, NEG)
    m = jnp.max(s, axis=-1, keepdims=True)
    p = jnp.exp(s - m)
    l = jnp.sum(p, axis=-1, keepdims=True)
    pv = jnp.dot(p.astype(BF16), v, preferred_element_type=F32)
    return pv / l, m + jnp.log(l)


def _attn_seq_kernel(q_ref, kp_ref, kc_ref, vp_ref, vc_ref, o_ref, kwin, vwin, og_scr, lse_scr):
    n = pl.program_id(2)
    blk, span = ATT_BLOCK, ATT_SPAN
    kwin[0:span, :] = kp_ref[...]
    kwin[span:, :] = kc_ref[...]
    vwin[0:span, :] = vp_ref[...]
    vwin[span:, :] = vc_ref[...]
    row = lax.broadcasted_iota(jnp.int32, (blk, 2 * blk), 0)
    col = lax.broadcasted_iota(jnp.int32, (blk, 2 * blk), 1)
    rel = row + blk - col
    band = jnp.where(rel >= 0, jnp.where(rel <= blk, 1, 0), 0)
    band_no_prev = jnp.where(col >= blk, band, 0)
    low_half = lax.broadcasted_iota(jnp.int32, (blk, LANES), 1) < HEAD_DIM
    zero = jnp.zeros((blk, LANES), BF16)

    for gi, (_, d) in enumerate(DILATED_GROUPS):
        shift = d.bit_length() - 1

        def unit(u, carry, gi=gi, d=d, shift=shift):
            res = u & (d - 1)
            sub = u >> shift
            q0 = sub * (blk * d) + res
            k0 = span + q0 - blk * d
            if d == 1:
                q0, k0 = pl.multiple_of(q0, blk), pl.multiple_of(k0, blk)
            qu = q_ref[pl.ds(q0, blk, stride=d), :].astype(BF16)
            ku = kwin[pl.ds(k0, 2 * blk, stride=d), :].astype(BF16)
            vu = vwin[pl.ds(k0, 2 * blk, stride=d), :].astype(BF16)
            first = jnp.logical_and(n == 0, sub == 0)
            mask = jnp.where(first, band_no_prev, band) > 0
            o_a, lse_a = _softmax_pv(jnp.where(low_half, qu, zero), ku, vu, mask)
            o_b, lse_b = _softmax_pv(jnp.where(low_half, zero, qu), ku, vu, mask)
            og_scr[gi, pl.ds(q0, blk, stride=d), :] = jnp.where(low_half, o_a, o_b)
            lse_scr[gi, pl.ds(q0, blk, stride=d), :] = jnp.where(low_half, lse_a, lse_b)
            return carry

        lax.fori_loop(0, span // blk, unit, 0)

    l1, l2, l3 = lse_scr[0], lse_scr[1], lse_scr[2]
    lmax = jnp.maximum(jnp.maximum(l1, l2), l3)
    e1, e2, e3 = jnp.exp(l1 - lmax), jnp.exp(l2 - lmax), jnp.exp(l3 - lmax)
    o_ref[0] = (e1 * og_scr[0] + e2 * og_scr[1] + e3 * og_scr[2]) / (e1 + e2 + e3)


def _attn_seq(q, k, v):
    _, b, s, _ = q.shape
    span = ATT_SPAN
    assert s % span == 0 and len(DILATED_GROUPS) == 3
    cur = pl.BlockSpec((None, None, span, LANES), lambda bi, p, n: (p, bi, n, 0))
    prev = pl.BlockSpec((None, None, span, LANES), lambda bi, p, n: (p, bi, jnp.maximum(n - 1, 0), 0))
    return pl.pallas_call(
        _attn_seq_kernel,
        grid=(b, HEAD_PAIRS, s // span),
        in_specs=[cur, prev, cur, prev, cur],
        out_specs=pl.BlockSpec((1, span, LANES), lambda bi, p, n: (bi, n, p)),
        out_shape=jax.ShapeDtypeStruct((b, s, D_ATT), F32),
        scratch_shapes=[pltpu.VMEM((2 * span, LANES), F32), pltpu.VMEM((2 * span, LANES), F32),
                        pltpu.VMEM((len(DILATED_GROUPS), span, LANES), F32),
                        pltpu.VMEM((len(DILATED_GROUPS), span, LANES), F32)],
        compiler_params=_cparams("parallel", "parallel", "arbitrary"),
        name="attn_seq",
    )(q, k, k, v, v)


def _decode_multiplicity(n_past, t_new):
    j = np.arange(t_new)[:, None]
    k = np.arange(n_past + LANES)[None, :]
    delta = n_past + j - k
    c = np.zeros((t_new, n_past + LANES), np.float32)
    for w, d in DILATED_GROUPS:
        c += ((delta >= 0) & (delta <= w) & (delta % d == 0) & (k < n_past + t_new)).astype(np.float32)
    return c[:, :n_past], c[:, n_past:]


def _attn_decode_kernel(q_ref, kn_ref, vn_ref, kt_ref, vt_ref, cold_ref, cnew_ref, o_ref):
    t_new = q_ref.shape[1]
    c_old, c_new = cold_ref[...], cnew_ref[...]
    zpad = jnp.zeros((LANES - t_new, LANES), F32)
    contract_last = (((1,), (1,)), ((), ()))
    for p in range(HEAD_PAIRS):
        qp = q_ref[p]
        knp = jnp.concatenate([kn_ref[p], zpad], axis=0)
        vnp = jnp.concatenate([vn_ref[p], zpad], axis=0)
        outs = []
        for half in range(2):
            h = 2 * p + half
            sl = slice(half * HEAD_DIM, (half + 1) * HEAD_DIM)
            qh = qp[:, sl].astype(BF16)
            s_old = jnp.dot(qh, kt_ref[0, 0, h].astype(BF16), preferred_element_type=F32)
            s_new = lax.dot_general(qh, knp[:, sl].astype(BF16), contract_last, preferred_element_type=F32)
            s_old = jnp.where(c_old > 0, s_old, NEG)
            s_new = jnp.where(c_new > 0, s_new, NEG)
            m = jnp.maximum(jnp.max(s_old, axis=-1, keepdims=True), jnp.max(s_new, axis=-1, keepdims=True))
            p_old = c_old * jnp.exp(s_old - m)
            p_new = c_new * jnp.exp(s_new - m)
            l = jnp.sum(p_old, axis=-1, keepdims=True) + jnp.sum(p_new, axis=-1, keepdims=True)
            o = lax.dot_general(p_old.astype(BF16), vt_ref[0, 0, h].astype(BF16), contract_last,
                                preferred_element_type=F32)
            o += jnp.dot(p_new.astype(BF16), vnp[:, sl].astype(BF16), preferred_element_type=F32)
            outs.append(o / l)
        o_ref[:, p * LANES:(p + 1) * LANES] = jnp.concatenate(outs, axis=1)


def _attn_decode(q, k_new, v_new, cache_kt, cache_vt, layer, t_new):
    _, rows, _ = q.shape
    _, b, h, e, n_past = cache_kt.shape
    c_old, c_new = _decode_multiplicity(n_past, t_new)
    new = pl.BlockSpec((HEAD_PAIRS, t_new, LANES), lambda i: (0, i, 0))
    past = pl.BlockSpec((1, 1, h, e, n_past), lambda i: (layer, i, 0, 0, 0))
    return pl.pallas_call(
        _attn_decode_kernel,
        grid=(b,),
        in_specs=[new, new, new, past, past, _const_spec((t_new, n_past)), _const_spec((t_new, LANES))],
        out_specs=pl.BlockSpec((t_new, D_ATT), lambda i: (i, 0)),
        out_shape=jax.ShapeDtypeStruct((rows, D_ATT), F32),
        compiler_params=_cparams("parallel"),
        name="attn_decode",
    )(q, k_new, v_new, cache_kt, cache_vt, jnp.asarray(c_old), jnp.asarray(c_new))


def _ssm_prep_kernel(are_ref, aim_ref, ldt_ref, bre_ref, bim_ref, pwre_ref, pwim_ref, bbre_ref, bbim_ref):
    a_re, a_im = are_ref[...], aim_ref[...]
    dt = jnp.exp(ldt_ref[...])
    steps = (lax.broadcasted_iota(jnp.int32, pwre_ref.shape, 0) + 1).astype(F32)
    mag = jnp.exp(steps * (dt * a_re))
    ang = steps * (dt * a_im)
    pw_re = mag * jnp.cos(ang)
    pw_im = mag * jnp.sin(ang)
    pwre_ref[...] = pw_re
    pwim_ref[...] = pw_im
    ab_re, ab_im = pw_re[0:1], pw_im[0:1]
    den = a_re * a_re + a_im * a_im
    n_re = ab_re - 1.0
    co_re = (n_re * a_re + ab_im * a_im) / den
    co_im = (ab_im * a_re - n_re * a_im) / den
    b_re, b_im = bre_ref[...], bim_ref[...]
    bbre_ref[...] = co_re * b_re - co_im * b_im
    bbim_ref[...] = co_re * b_im + co_im * b_re


def _ssm_prep(log_dt, a_re, a_im, b_re, b_im):
    g, p = a_re.shape
    n = g * p
    flat = lambda t: t.reshape(1, n)
    cmaj = lambda t: jnp.transpose(t, (2, 0, 1)).reshape(SSM_GROUP, n)
    pw_re, pw_im, bb_re, bb_im = pl.pallas_call(
        _ssm_prep_kernel,
        out_shape=[jax.ShapeDtypeStruct((SUBLANES, n), F32)] * 2 + [jax.ShapeDtypeStruct((SSM_GROUP, n), F32)] * 2,
        name="ssm_prep",
    )(flat(a_re), flat(a_im), flat(jnp.repeat(log_dt, p)), cmaj(b_re), cmaj(b_im))
    eye = jnp.eye(g, dtype=F32)

    def expand(bb):
        return (bb.reshape(1, SSM_GROUP, g, p) * eye[:, None, :, None]).reshape(g * SSM_GROUP, n)

    b_big = jnp.concatenate([expand(bb_re), expand(bb_im)], axis=1).astype(BF16)
    return pw_re, pw_im, b_big


def _c_big(c_re, c_im):
    g, c, p = c_re.shape
    eye = jnp.eye(g, dtype=F32)

    def expand(t):
        return (jnp.transpose(t, (0, 2, 1))[:, :, None, :] * eye[:, None, :, None]).reshape(g * p, g * c)

    return jnp.concatenate([expand(c_re), -expand(c_im)], axis=0).astype(BF16)


def _ssm_kernel(u_ref, h0re_ref, h0im_ref, pwre_ref, pwim_ref, bbig_ref, cbig_ref, d_ref, wglu_ref, bglu_ref,
                out_ref, hre_ref, him_ref, bu_scr, h_scr, cre_scr, cim_scr, *, per_slab_state):
    i = pl.program_id(1)
    tt = u_ref.shape[1]
    n = pwre_ref.shape[1]
    u = u_ref[0]
    bu_scr[...] = jnp.dot(u.astype(BF16), bbig_ref[...], preferred_element_type=F32)

    if not per_slab_state:
        @pl.when(i == 0)
        def _():
            cre_scr[...] = h0re_ref[0]
            cim_scr[...] = h0im_ref[0]

    pw_re, pw_im = pwre_ref[...], pwim_ref[...]
    rows = lax.broadcasted_iota(jnp.int32, (SUBLANES, n), 0)
    step_coef = []
    for sh in (1, 2, 4):
        keep = rows >= sh
        step_coef.append((sh, jnp.where(keep, jnp.broadcast_to(pw_re[sh - 1:sh], (SUBLANES, n)), 0.0),
                          jnp.where(keep, jnp.broadcast_to(pw_im[sh - 1:sh], (SUBLANES, n)), 0.0)))

    def slab(j, carry):
        c_re, c_im = carry
        r0 = pl.multiple_of(j * SUBLANES, SUBLANES)
        if per_slab_state:
            c_re = h0re_ref[0, pl.ds(j, 1), :]
            c_im = h0im_ref[0, pl.ds(j, 1), :]
        x_re = bu_scr[pl.ds(r0, SUBLANES), 0:n]
        x_im = bu_scr[pl.ds(r0, SUBLANES), n:2 * n]
        for sh, a_re, a_im in step_coef:
            s_re = pltpu.roll(x_re, sh, 0)
            s_im = pltpu.roll(x_im, sh, 0)
            x_re, x_im = x_re + a_re * s_re - a_im * s_im, x_im + a_re * s_im + a_im * s_re
        x_re, x_im = x_re + pw_re * c_re - pw_im * c_im, x_im + pw_re * c_im + pw_im * c_re
        h_scr[pl.ds(r0, SUBLANES), 0:n] = x_re
        h_scr[pl.ds(r0, SUBLANES), n:2 * n] = x_im
        c_re, c_im = x_re[SUBLANES - 1:SUBLANES], x_im[SUBLANES - 1:SUBLANES]
        if per_slab_state:
            hre_ref[0, pl.ds(j, 1), :] = c_re
            him_ref[0, pl.ds(j, 1), :] = c_im
        return c_re, c_im

    if per_slab_state:
        init = (jnp.zeros((1, n), F32), jnp.zeros((1, n), F32))
    else:
        init = (cre_scr[...], cim_scr[...])
    c_re, c_im = lax.fori_loop(0, tt // SUBLANES, slab, init)
    if not per_slab_state:
        cre_scr[...] = c_re
        cim_scr[...] = c_im
        hre_ref[0] = c_re
        him_ref[0] = c_im

    y = jnp.dot(h_scr[...].astype(BF16), cbig_ref[...], preferred_element_type=F32) + d_ref[...] * u
    gl = jax.nn.gelu(y)
    gate = jnp.dot(gl.astype(BF16), wglu_ref[...], preferred_element_type=F32) + bglu_ref[...]
    out_ref[0] = gl * jax.nn.sigmoid(gate)


def _ssm(u, h0_re, h0_im, pw_re, pw_im, b_big, c_big, d_skip, w_glu_bf, b_glu, tile, per_slab_state):
    b, t, d_ssm = u.shape
    ns, n = h0_re.shape[1], h0_re.shape[2]
    st = pl.BlockSpec((1, ns, n), lambda bi, i: (bi, 0, 0))
    tok = pl.BlockSpec((1, tile, d_ssm), lambda bi, i: (bi, i, 0))
    return pl.pallas_call(
        functools.partial(_ssm_kernel, per_slab_state=per_slab_state),
        grid=(b, t // tile),
        in_specs=[tok, st, st, _const_spec((SUBLANES, n)), _const_spec((SUBLANES, n)),
                  _const_spec((d_ssm, 2 * n)), _const_spec((2 * n, d_ssm)), _const_spec((1, d_ssm)),
                  _const_spec((d_ssm, d_ssm)), _const_spec((1, d_ssm))],
        out_specs=[tok, st, st],
        out_shape=[jax.ShapeDtypeStruct((b, t, d_ssm), F32), jax.ShapeDtypeStruct((b, ns, n), F32),
                   jax.ShapeDtypeStruct((b, ns, n), F32)],
        scratch_shapes=[pltpu.VMEM((tile, 2 * n), F32), pltpu.VMEM((tile, 2 * n), F32),
                        pltpu.VMEM((1, n), F32), pltpu.VMEM((1, n), F32)],
        compiler_params=_cparams("arbitrary", "arbitrary"),
        name="ssm_slab" if per_slab_state else "ssm_seq",
    )(u, h0_re, h0_im, pw_re, pw_im, b_big, c_big, d_skip, w_glu_bf, b_glu)


def _pool_kernel(halo_ref, u_ref, w_ref, scale_ref, o_ref, *, start_pos, zero_first_halo):
    i = pl.program_id(1)
    tt, c = u_ref.shape[1], u_ref.shape[2]
    halo = halo_ref[0]
    if zero_first_halo:
        halo = jnp.where(i > 0, halo, 0.0)
    u = u_ref[0]
    xe = jnp.concatenate([halo, u], axis=0)
    sums, acc, sh = [], xe, 1
    for _ in POOL_WINDOWS:
        acc = acc + pltpu.roll(acc, sh, 0)
        sums.append(acc[POOL_HALO:])
        sh *= 2
    lane = lax.broadcasted_iota(jnp.int32, (tt, c), 1)
    group_width = c // len(POOL_WINDOWS)
    win, wsz = sums[-1], jnp.full((tt, c), POOL_WINDOWS[-1], jnp.int32)
    for gi in range(len(POOL_WINDOWS) - 2, -1, -1):
        in_group = lane < (gi + 1) * group_width
        win = jnp.where(in_group, sums[gi], win)
        wsz = jnp.where(in_group, POOL_WINDOWS[gi], wsz)
    pos = start_pos + i * tt + lax.broadcasted_iota(jnp.int32, (tt, c), 0)
    cnt = jnp.minimum(pos + 1, wsz).astype(F32)
    pooled = win / cnt - u
    o_ref[0] = jnp.dot(pooled.astype(BF16), w_ref[...], preferred_element_type=F32) * scale_ref[...]


def _pool(u, prefix16, w_big_bf, scale, tile, start_pos):
    b, t, c = u.shape
    tok = pl.BlockSpec((1, tile, c), lambda bi, i: (bi, i, 0))
    if prefix16 is None:
        per = tile // POOL_HALO
        halo_src = u
        halo = pl.BlockSpec((1, POOL_HALO, c), lambda bi, i: (bi, jnp.maximum(i * per - 1, 0), 0))
    else:
        assert t == tile
        halo_src = prefix16
        halo = pl.BlockSpec((1, POOL_HALO, c), lambda bi, i: (bi, 0, 0))
    return pl.pallas_call(
        functools.partial(_pool_kernel, start_pos=start_pos, zero_first_halo=prefix16 is None),
        grid=(b, t // tile),
        in_specs=[halo, tok, _const_spec((c, c)), _const_spec((1, c))],
        out_specs=tok,
        out_shape=jax.ShapeDtypeStruct((b, t, c), F32),
        compiler_params=_cparams("parallel", "arbitrary"),
        name="pool_seq" if prefix16 is None else "pool_prefix",
    )(halo_src, u, w_big_bf, scale)


def _ffn(x1, g2_ref, wup_ref, cw_ref, cb_ref, wdown_ref, acc_scr, shifted_rows):
    hn2 = _rms(x1, g2_ref[...]).astype(BF16)
    acc_scr[...] = jnp.zeros_like(acc_scr)
    n_chunks = wup_ref.shape[1]

    def chunk(c, carry):
        act = []
        for part in range(2):
            up = jnp.dot(hn2, wup_ref[part, c], preferred_element_type=F32)
            up1, up2 = shifted_rows(up, part, c)
            cw = cw_ref[part, c]
            act.append(cb_ref[part, c] + up2 * cw[0:1] + up1 * cw[1:2] + up * cw[2:3])
        h = (jax.nn.silu(act[1]) * act[0]).astype(BF16)
        acc_scr[...] += jnp.dot(h, wdown_ref[c], preferred_element_type=F32)
        return carry

    lax.fori_loop(0, n_chunks, chunk, 0)
    return x1 + acc_scr[...]


def _mix_out(att_ref, ssm_ref, pool_ref, x_ref, gatt_ref, gssm_ref, gpool_ref, wout_ref):
    d_ssm = ssm_ref.shape[-1]
    mix = jnp.dot(_rms(att_ref[0], gatt_ref[...]).astype(BF16), wout_ref[0:D_ATT, :], preferred_element_type=F32)
    mix += jnp.dot(_rms(ssm_ref[0], gssm_ref[...]).astype(BF16), wout_ref[D_ATT:D_ATT + d_ssm, :],
                   preferred_element_type=F32)
    mix += jnp.dot(_rms(pool_ref[0], gpool_ref[...]).astype(BF16), wout_ref[D_ATT + d_ssm:, :],
                   preferred_element_type=F32)
    return x_ref[0] + mix


def _out_seq_kernel(att_ref, ssm_ref, pool_ref, x_ref,
                    gatt_ref, gssm_ref, gpool_ref, wout_ref, g2_ref, wup_ref, cw_ref, cb_ref, wdown_ref, gf_ref,
                    *rest, final):
    if final:
        xo_ref, y_ref, conv_ref, acc_scr, tail_scr = rest
    else:
        xo_ref, conv_ref, acc_scr, tail_scr = rest
    i = pl.program_id(1)
    tt = x_ref.shape[1]

    @pl.when(i == 0)
    def _():
        tail_scr[...] = jnp.zeros_like(tail_scr)

    x1 = _mix_out(att_ref, ssm_ref, pool_ref, x_ref, gatt_ref, gssm_ref, gpool_ref, wout_ref)

    def shifted_rows(up, part, c):
        ext = jnp.concatenate([tail_scr[part, c], up], axis=0)
        tail_scr[part, c] = up[tt - SUBLANES:]
        conv_ref[0, part, c] = up[tt - (CONV_WIDTH - 1):]
        return pltpu.roll(ext, 1, 0)[SUBLANES:], pltpu.roll(ext, 2, 0)[SUBLANES:]

    x2 = _ffn(x1, g2_ref, wup_ref, cw_ref, cb_ref, wdown_ref, acc_scr, shifted_rows)
    xo_ref[0] = x2
    if final:
        y_ref[0] = _rms(x2, gf_ref[...])


def _out_slab_kernel(att_ref, ssm_ref, pool_ref, x_ref, p1_ref, p2_ref,
                     gatt_ref, gssm_ref, gpool_ref, wout_ref, g2_ref, wup_ref, cw_ref, cb_ref, wdown_ref, gf_ref,
                     *rest, final):
    if final:
        xo_ref, y_ref, up_ref, acc_scr = rest
    else:
        xo_ref, up_ref, acc_scr = rest
    tt = x_ref.shape[1]
    x1 = _mix_out(att_ref, ssm_ref, pool_ref, x_ref, gatt_ref, gssm_ref, gpool_ref, wout_ref)
    t_in = lax.broadcasted_iota(jnp.int32, (tt, FF_CHUNK), 0) & (SUBLANES - 1)

    def shifted_rows(up, part, c):
        up_ref[part, c] = up
        return (jnp.where(t_in >= 1, pltpu.roll(up, 1, 0), p1_ref[part, c]),
                jnp.where(t_in >= 2, pltpu.roll(up, 2, 0), p2_ref[part, c]))

    x2 = _ffn(x1, g2_ref, wup_ref, cw_ref, cb_ref, wdown_ref, acc_scr, shifted_rows)
    xo_ref[0] = x2
    if final:
        y_ref[0] = _rms(x2, gf_ref[...])


def _weight_specs(dm, d_ssm, d_pool, n_chunks):
    return [_const_spec((1, D_ATT)), _const_spec((1, d_ssm)), _const_spec((1, d_pool)),
            _const_spec((dm, dm)), _const_spec((1, dm)),
            _const_spec((2, n_chunks, dm, FF_CHUNK)), _const_spec((2, n_chunks, CONV_WIDTH, FF_CHUNK)),
            _const_spec((2, n_chunks, 1, FF_CHUNK)), _const_spec((n_chunks, FF_CHUNK, dm)), _const_spec((1, dm))]


def _out_seq(att, ssm_out, pool_out, x, lw, tile, final):
    b, t, dm = x.shape
    d_ssm, d_pool = ssm_out.shape[-1], pool_out.shape[-1]
    n_chunks = lw['w_down'].shape[0]
    tok = lambda c: pl.BlockSpec((1, tile, c), lambda bi, i: (bi, i, 0))
    conv_spec = pl.BlockSpec((1, 2, n_chunks, CONV_WIDTH - 1, FF_CHUNK), lambda bi, i: (bi, 0, 0, 0, 0))
    out_specs = [tok(dm)] + ([tok(dm)] if final else []) + [conv_spec]
    out_shape = ([jax.ShapeDtypeStruct((b, t, dm), F32)] * (2 if final else 1)
                 + [jax.ShapeDtypeStruct((b, 2, n_chunks, CONV_WIDTH - 1, FF_CHUNK), F32)])
    res = pl.pallas_call(
        functools.partial(_out_seq_kernel, final=final),
        grid=(b, t // tile),
        in_specs=[tok(D_ATT), tok(d_ssm), tok(d_pool), tok(dm)] + _weight_specs(dm, d_ssm, d_pool, n_chunks),
        out_specs=out_specs,
        out_shape=out_shape,
        scratch_shapes=[pltpu.VMEM((tile, dm), F32), pltpu.VMEM((2, n_chunks, SUBLANES, FF_CHUNK), F32)],
        compiler_params=_cparams("parallel", "arbitrary"),
        name="out_seq",
    )(att, ssm_out, pool_out, x,
      lw['g_att'], lw['g_ssm'], lw['g_pool'], lw['w_out'], lw['g2'], lw['w_up'], lw['conv_w'], lw['conv_b'],
      lw['w_down'], lw['g_f'])
    conv_state = jnp.transpose(res[-1], (0, 3, 1, 2, 4)).reshape(b, CONV_WIDTH - 1, 2 * n_chunks * FF_CHUNK)
    return res[0], (res[1] if final else None), conv_state


def _out_slab(att, ssm_out, pool_out, x, conv_prefix, lw, final):
    _, r, dm = x.shape
    d_ssm, d_pool = ssm_out.shape[-1], pool_out.shape[-1]
    n_chunks = lw['w_down'].shape[0]
    b = conv_prefix.shape[0]
    t = r // b
    assert t == SUBLANES and CONV_WIDTH == 3

    def chunked(rows):
        return jnp.transpose(rows.reshape(r, 2, n_chunks, FF_CHUNK), (1, 2, 0, 3))

    zeros = jnp.zeros((b, 1, conv_prefix.shape[-1]), F32)
    p1 = jnp.concatenate([conv_prefix[:, 1:2]] + [zeros] * (t - 1), axis=1)
    p2 = jnp.concatenate([conv_prefix[:, 0:1], conv_prefix[:, 1:2]] + [zeros] * (t - 2), axis=1)
    p1, p2 = chunked(p1.reshape(r, -1)), chunked(p2.reshape(r, -1))
    full = lambda shape: pl.BlockSpec(shape, lambda: (0,) * len(shape))
    up_shape = (2, n_chunks, r, FF_CHUNK)
    out_specs = [full((1, r, dm))] + ([full((1, r, dm))] if final else []) + [full(up_shape)]
    out_shape = ([jax.ShapeDtypeStruct((1, r, dm), F32)] * (2 if final else 1)
                 + [jax.ShapeDtypeStruct(up_shape, F32)])
    wspecs = [full(s.block_shape) for s in _weight_specs(dm, d_ssm, d_pool, n_chunks)]
    res = pl.pallas_call(
        functools.partial(_out_slab_kernel, final=final),
        in_specs=[full((1, r, D_ATT)), full((1, r, d_ssm)), full((1, r, d_pool)), full((1, r, dm)),
                  full(up_shape), full(up_shape)] + wspecs,
        out_specs=out_specs,
        out_shape=out_shape,
        scratch_shapes=[pltpu.VMEM((r, dm), F32)],
        compiler_params=pltpu.CompilerParams(vmem_limit_bytes=VMEM_LIMIT),
        name="out_slab",
    )(att, ssm_out, pool_out, x, p1, p2,
      lw['g_att'], lw['g_ssm'], lw['g_pool'], lw['w_out'], lw['g2'], lw['w_up'], lw['conv_w'], lw['conv_b'],
      lw['w_down'], lw['g_f'])
    up = jnp.transpose(res[-1], (2, 0, 1, 3)).reshape(b, t, 2 * n_chunks * FF_CHUNK)
    return res[0], (res[1] if final else None), up[:, t - (CONV_WIDTH - 1):]


def _block_diag(w):
    g, a, b = w.shape
    eye = jnp.eye(g, dtype=w.dtype)
    return (w[:, :, None, :] * eye[:, None, :, None]).reshape(g * a, g * b)


def _layer_weights(i, p):
    dm = p['w_in'].shape[1]
    d_ff = p['w_down'].shape[1]
    n_chunks = d_ff // FF_CHUNK
    assert n_chunks * FF_CHUNK == d_ff
    row = lambda t: t.reshape(1, -1)
    chunk_cols = lambda t: jnp.transpose(t.reshape(t.shape[0], 2, n_chunks, FF_CHUNK), (1, 2, 0, 3))
    return {
        'g1': row(p['norm1_g'][i]), 'w_in': p['w_in'][i].astype(BF16),
        'g_att': row(p['out_norm_att'][i]), 'g_ssm': row(p['out_norm_ssm'][i]),
        'g_pool': row(p['out_norm_pool'][i]),
        'w_out': p['w_out'][i].astype(BF16), 'g2': row(p['norm2_g'][i]),
        'w_up': chunk_cols(p['w_up'][i].astype(BF16)),
        'conv_w': chunk_cols(p['conv_w'][i]), 'conv_b': chunk_cols(row(p['conv_b'][i])),
        'w_down': p['w_down'][i].astype(BF16).reshape(n_chunks, FF_CHUNK, dm),
        'g_f': row(p['norm_f_g']),
        'ssm_d': row(p['ssm_d'][i]), 'w_glu': p['ssm_w_glu'][i].astype(BF16), 'b_glu': row(p['ssm_b_glu'][i]),
        'pool_w': _block_diag(p['pool_w'][i]).astype(BF16), 'pool_scale': row(p['pool_scale'][i]),
    }


def kernel(x_prompt, x_sample, cache_k, cache_v, state_ssm_re, state_ssm_im, state_pool, state_conv, norm1_g, w_in, ssm_log_dt, ssm_a_re, ssm_a_im, ssm_b_re, ssm_b_im, ssm_c_re, ssm_c_im, ssm_d, ssm_w_glu, ssm_b_glu, pool_w, pool_scale, out_norm_att, out_norm_ssm, out_norm_pool, w_out, norm2_g, w_up, conv_w, conv_b, w_down, norm_f_g):
    params = dict(norm1_g=norm1_g, w_in=w_in, ssm_d=ssm_d, ssm_w_glu=ssm_w_glu, ssm_b_glu=ssm_b_glu,
                  pool_w=pool_w, pool_scale=pool_scale, out_norm_att=out_norm_att, out_norm_ssm=out_norm_ssm,
                  out_norm_pool=out_norm_pool, w_out=w_out, norm2_g=norm2_g, w_up=w_up, conv_w=conv_w,
                  conv_b=conv_b, w_down=w_down, norm_f_g=norm_f_g)
    depth = w_in.shape[0]
    bp, sp, dm = x_prompt.shape
    bs, ts, _ = x_sample.shape
    n_past = cache_k.shape[2]
    grp = ssm_a_re.shape[1]
    n_state = grp * ssm_a_re.shape[2]
    d_ssm, d_pool = ssm_d.shape[1], pool_scale.shape[1]
    pool_buf = state_pool.shape[2]
    n_keep = min(n_past, sp)
    assert ts == SUBLANES

    cache_kt = jnp.transpose(cache_k.astype(F32), (0, 1, 3, 4, 2))
    cache_vt = jnp.transpose(cache_v.astype(F32), (0, 1, 3, 4, 2))

    xp = x_prompt.astype(F32)
    xs = x_sample.astype(F32).reshape(1, bs * ts, dm)
    yp = ys = None
    st_p, st_s = [], []
    for i in range(depth):
        lw = _layer_weights(i, params)
        final = i == depth - 1
        pw_re, pw_im, b_big = _ssm_prep(ssm_log_dt[i], ssm_a_re[i], ssm_a_im[i], ssm_b_re[i], ssm_b_im[i])
        c_big = _c_big(ssm_c_re[i], ssm_c_im[i])

        q, k, v, u_ssm, u_pool = _in_proj(xp.reshape(bp * sp, dm), lw['g1'], lw['w_in'], d_ssm, d_pool, 512)
        seq = lambda t: t.reshape(bp, sp, t.shape[-1])
        pair_seq = lambda t: t.reshape(HEAD_PAIRS, bp, sp, LANES)
        att = _attn_seq(pair_seq(q), pair_seq(k), pair_seq(v))
        zero_h = jnp.zeros((bp, 1, n_state), F32)
        ssm_out, h_re, h_im = _ssm(seq(u_ssm), zero_h, zero_h, pw_re, pw_im, b_big, c_big, lw['ssm_d'],
                                   lw['w_glu'], lw['b_glu'], 512, False)
        pool_out = _pool(seq(u_pool), None, lw['pool_w'], lw['pool_scale'], 512, 0)
        xp, yp_i, conv_p = _out_seq(att, ssm_out, pool_out, xp, lw, 512, final)
        yp = yp_i if final else yp
        st_p.append((_pairs_to_heads(pair_seq(k)[:, :, sp - n_keep:]), _pairs_to_heads(pair_seq(v)[:, :, sp - n_keep:]),
                     h_re.reshape(bp, grp, -1), h_im.reshape(bp, grp, -1),
                     seq(u_pool)[:, sp - pool_buf:], conv_p))

        q, k, v, u_ssm, u_pool = _in_proj(xs[0], lw['g1'], lw['w_in'], d_ssm, d_pool, bs * ts)
        per_seq = lambda t: t.reshape(bs, ts, t.shape[-1])
        att = _attn_decode(q, k, v, cache_kt, cache_vt, i, ts)
        ssm_out, h_re, h_im = _ssm(u_ssm[None], state_ssm_re[i].reshape(1, bs, n_state),
                                   state_ssm_im[i].reshape(1, bs, n_state), pw_re, pw_im, b_big, c_big,
                                   lw['ssm_d'], lw['w_glu'], lw['b_glu'], bs * ts, True)
        prefix16 = jnp.concatenate([jnp.zeros((bs, POOL_HALO - pool_buf, d_pool), F32),
                                    state_pool[i].astype(F32)], axis=1)
        pool_out = _pool(per_seq(u_pool), prefix16, lw['pool_w'], lw['pool_scale'], ts, PAST_LEN)
        xs, ys_i, conv_s = _out_slab(att[None], ssm_out, pool_out.reshape(1, bs * ts, d_pool),
                                     xs, state_conv[i].astype(F32), lw, final)
        ys = ys_i if final else ys
        pool_state = jnp.concatenate([state_pool[i].astype(F32), per_seq(u_pool)], axis=1)[:, -pool_buf:]
        new_heads = lambda t: _pairs_to_heads(t.reshape(HEAD_PAIRS, bs, ts, LANES))
        st_s.append((new_heads(k), new_heads(v), h_re.reshape(bs, grp, -1), h_im.reshape(bs, grp, -1),
                     pool_state, conv_s))

    stack = lambda sts, j: jnp.stack([s[j] for s in sts], 0)
    return ((yp.astype(x_prompt.dtype), ys.reshape(bs, ts, dm).astype(x_sample.dtype))
            + tuple(stack(st_p, j) for j in range(6)) + tuple(stack(st_s, j) for j in range(6)))
```

```python
import functools

import jax
import jax.numpy as jnp
import numpy as np
from jax import lax
from jax.experimental import pallas as pl
from jax.experimental.pallas import tpu as pltpu

F32 = jnp.float32
BF16 = jnp.bfloat16

N_HEADS = 8
HEAD_DIM = 64
D_ATT = N_HEADS * HEAD_DIM
DILATED_GROUPS = ((128, 1), (512, 4), (2048, 16))
ATT_BLOCK = 128
ATT_SPAN = ATT_BLOCK * max(d for _, d in DILATED_GROUPS)
SSM_GROUP = 16
SSM_STATE = 64
POOL_WINDOWS = (2, 4, 8, 16)
POOL_HALO = 16
CONV_WIDTH = 3
PAST_LEN = 16384
EPS = 1e-6
NEG = -1e30

SUBLANES = 8
LANES = 128
HEAD_PAIRS = D_ATT // LANES
FF_CHUNK = 256
VMEM_LIMIT = 56 * 1024 * 1024


def _cparams(*sem):
    return pltpu.CompilerParams(dimension_semantics=sem, vmem_limit_bytes=VMEM_LIMIT)


def _rms(x, g):
    return x * lax.rsqrt(jnp.mean(x * x, axis=-1, keepdims=True) + EPS) * g


def _const_spec(shape):
    nd = len(shape)
    return pl.BlockSpec(shape, lambda *_: (0,) * nd, pipeline_mode=pl.Buffered(1))


def _in_proj_kernel(x_ref, g_ref, w_ref, q_ref, k_ref, v_ref, us_ref, up_ref):
    hn = _rms(x_ref[...], g_ref[...]).astype(BF16)
    z = jnp.dot(hn, w_ref[...], preferred_element_type=F32)
    d_ssm = us_ref.shape[-1]
    for p in range(HEAD_PAIRS):
        lo = p * LANES
        q_ref[p] = z[:, lo:lo + LANES] * (HEAD_DIM ** -0.5)
        k_ref[p] = z[:, D_ATT + lo:D_ATT + lo + LANES]
        v_ref[p] = z[:, 2 * D_ATT + lo:2 * D_ATT + lo + LANES]
    us_ref[...] = z[:, 3 * D_ATT:3 * D_ATT + d_ssm]
    up_ref[...] = z[:, 3 * D_ATT + d_ssm:]


def _in_proj(x2d, g, w_bf, d_ssm, d_pool, tile):
    n, dm = x2d.shape
    d_in = w_bf.shape[1]
    row = lambda c: pl.BlockSpec((tile, c), lambda i: (i, 0))
    pairs = pl.BlockSpec((HEAD_PAIRS, tile, LANES), lambda i: (0, i, 0))
    return pl.pallas_call(
        _in_proj_kernel,
        grid=(n // tile,),
        in_specs=[row(dm), _const_spec((1, dm)), _const_spec((dm, d_in))],
        out_specs=[pairs] * 3 + [row(d_ssm), row(d_pool)],
        out_shape=[jax.ShapeDtypeStruct((HEAD_PAIRS, n, LANES), F32)] * 3
                  + [jax.ShapeDtypeStruct((n, d_ssm), F32), jax.ShapeDtypeStruct((n, d_pool), F32)],
        compiler_params=_cparams("parallel"),
        name="in_proj",
    )(x2d, g, w_bf)


def _pairs_to_heads(t):
    t = jnp.moveaxis(t, 0, -2)
    return t.reshape(t.shape[:-2] + (N_HEADS, HEAD_DIM))


def _softmax_pv(qm, k, v, mask):
    s = lax.dot_general(qm, k, (((1,), (1,)), ((), ())), preferred_element_type=F32)
    s = jnp.where(mask, s, NEG)
    m = jnp.max(s, axis=-1, keepdims=True)
    p = jnp.exp(s - m)
    l = jnp.sum(p, axis=-1, keepdims=True)
    pv = jnp.dot(p.astype(BF16), v, preferred_element_type=F32)
    return pv / l, m + jnp.log(l)


def _attn_seq_kernel(q_ref, kp_ref, kc_ref, vp_ref, vc_ref, o_ref, kwin, vwin, og_scr, lse_scr):
    n = pl.program_id(2)
    blk, span = ATT_BLOCK, ATT_SPAN
    kwin[0:span, :] = kp_ref[...]
    kwin[span:, :] = kc_ref[...]
    vwin[0:span, :] = vp_ref[...]
    vwin[span:, :] = vc_ref[...]
    row = lax.broadcasted_iota(jnp.int32, (blk, 2 * blk), 0)
    col = lax.broadcasted_iota(jnp.int32, (blk, 2 * blk), 1)
    rel = row + blk - col
    band = jnp.where(rel >= 0, jnp.where(rel <= blk, 1, 0), 0)
    band_no_prev = jnp.where(col >= blk, band, 0)
    low_half = lax.broadcasted_iota(jnp.int32, (blk, LANES), 1) < HEAD_DIM
    zero = jnp.zeros((blk, LANES), BF16)

    for gi, (_, d) in enumerate(DILATED_GROUPS):
        shift = d.bit_length() - 1

        def unit(u, carry, gi=gi, d=d, shift=shift):
            res = u & (d - 1)
            sub = u >> shift
            q0 = sub * (blk * d) + res
            k0 = span + q0 - blk * d
            if d == 1:
                q0, k0 = pl.multiple_of(q0, blk), pl.multiple_of(k0, blk)
            qu = q_ref[pl.ds(q0, blk, stride=d), :].astype(BF16)
            ku = kwin[pl.ds(k0, 2 * blk, stride=d), :].astype(BF16)
            vu = vwin[pl.ds(k0, 2 * blk, stride=d), :].astype(BF16)
            first = jnp.logical_and(n == 0, sub == 0)
            mask = jnp.where(first, band_no_prev, band) > 0
            o_a, lse_a = _softmax_pv(jnp.where(low_half, qu, zero), ku, vu, mask)
            o_b, lse_b = _softmax_pv(jnp.where(low_half, zero, qu), ku, vu, mask)
            og_scr[gi, pl.ds(q0, blk, stride=d), :] = jnp.where(low_half, o_a, o_b)
            lse_scr[gi, pl.ds(q0, blk, stride=d), :] = jnp.where(low_half, lse_a, lse_b)
            return carry

        lax.fori_loop(0, span // blk, unit, 0, unroll=8)

    l1, l2, l3 = lse_scr[0], lse_scr[1], lse_scr[2]
    lmax = jnp.maximum(jnp.maximum(l1, l2), l3)
    e1, e2, e3 = jnp.exp(l1 - lmax), jnp.exp(l2 - lmax), jnp.exp(l3 - lmax)
    o_ref[0] = (e1 * og_scr[0] + e2 * og_scr[1] + e3 * og_scr[2]) / (e1 + e2 + e3)


def _attn_seq(q, k, v):
    _, b, s, _ = q.shape
    span = ATT_SPAN
    assert s % span == 0 and len(DILATED_GROUPS) == 3
    cur = pl.BlockSpec((None, None, span, LANES), lambda bi, p, n: (p, bi, n, 0))
    prev = pl.BlockSpec((None, None, span, LANES), lambda bi, p, n: (p, bi, jnp.maximum(n - 1, 0), 0))
    return pl.pallas_call(
        _attn_seq_kernel,
        grid=(b, HEAD_PAIRS, s // span),
        in_specs=[cur, prev, cur, prev, cur],
        out_specs=pl.BlockSpec((1, span, LANES), lambda bi, p, n: (bi, n, p)),
        out_shape=jax.ShapeDtypeStruct((b, s, D_ATT), F32),
        scratch_shapes=[pltpu.VMEM((2 * span, LANES), F32), pltpu.VMEM((2 * span, LANES), F32),
                        pltpu.VMEM((len(DILATED_GROUPS), span, LANES), F32),
                        pltpu.VMEM((len(DILATED_GROUPS), span, LANES), F32)],
        compiler_params=_cparams("parallel", "parallel", "arbitrary"),
        name="attn_seq",
    )(q, k, k, v, v)


def _decode_multiplicity(n_past, t_new):
    j = np.arange(t_new)[:, None]
    k = np.arange(n_past + LANES)[None, :]
    delta = n_past + j - k
    c = np.zeros((t_new, n_past + LANES), np.float32)
    for w, d in DILATED_GROUPS:
        c += ((delta >= 0) & (delta <= w) & (delta % d == 0) & (k < n_past + t_new)).astype(np.float32)
    return c[:, :n_past], c[:, n_past:]


def _attn_decode_kernel(q_ref, kn_ref, vn_ref, kt_ref, vt_ref, cold_ref, cnew_ref, o_ref):
    t_new = q_ref.shape[1]
    c_old, c_new = cold_ref[...], cnew_ref[...]
    zpad = jnp.zeros((LANES - t_new, LANES), F32)
    contract_last = (((1,), (1,)), ((), ()))
    for p in range(HEAD_PAIRS):
        qp = q_ref[p]
        knp = jnp.concatenate([kn_ref[p], zpad], axis=0)
        vnp = jnp.concatenate([vn_ref[p], zpad], axis=0)
        outs = []
        for half in range(2):
            h = 2 * p + half
            sl = slice(half * HEAD_DIM, (half + 1) * HEAD_DIM)
            qh = qp[:, sl].astype(BF16)
            s_old = jnp.dot(qh, kt_ref[0, 0, h].astype(BF16), preferred_element_type=F32)
            s_new = lax.dot_general(qh, knp[:, sl].astype(BF16), contract_last, preferred_element_type=F32)
            s_old = jnp.where(c_old > 0, s_old, NEG)
            s_new = jnp.where(c_new > 0, s_new, NEG)
            m = jnp.maximum(jnp.max(s_old, axis=-1, keepdims=True), jnp.max(s_new, axis=-1, keepdims=True))
            p_old = c_old * jnp.exp(s_old - m)
            p_new = c_new * jnp.exp(s_new - m)
            l = jnp.sum(p_old, axis=-1, keepdims=True) + jnp.sum(p_new, axis=-1, keepdims=True)
            o = lax.dot_general(p_old.astype(BF16), vt_ref[0, 0, h].astype(BF16), contract_last,
                                preferred_element_type=F32)
            o += jnp.dot(p_new.astype(BF16), vnp[:, sl].astype(BF16), preferred_element_type=F32)
            outs.append(o / l)
        o_ref[:, p * LANES:(p + 1) * LANES] = jnp.concatenate(outs, axis=1)


def _attn_decode(q, k_new, v_new, cache_kt, cache_vt, layer, t_new):
    _, rows, _ = q.shape
    _, b, h, e, n_past = cache_kt.shape
    c_old, c_new = _decode_multiplicity(n_past, t_new)
    new = pl.BlockSpec((HEAD_PAIRS, t_new, LANES), lambda i: (0, i, 0))
    past = pl.BlockSpec((1, 1, h, e, n_past), lambda i: (layer, i, 0, 0, 0))
    return pl.pallas_call(
        _attn_decode_kernel,
        grid=(b,),
        in_specs=[new, new, new, past, past, _const_spec((t_new, n_past)), _const_spec((t_new, LANES))],
        out_specs=pl.BlockSpec((t_new, D_ATT), lambda i: (i, 0)),
        out_shape=jax.ShapeDtypeStruct((rows, D_ATT), F32),
        compiler_params=_cparams("parallel"),
        name="attn_decode",
    )(q, k_new, v_new, cache_kt, cache_vt, jnp.asarray(c_old), jnp.asarray(c_new))


def _ssm_prep_kernel(are_ref, aim_ref, ldt_ref, bre_ref, bim_ref, pwre_ref, pwim_ref, bbre_ref, bbim_ref):
    a_re, a_im = are_ref[...], aim_ref[...]
    dt = jnp.exp(ldt_ref[...])
    steps = (lax.broadcasted_iota(jnp.int32, pwre_ref.shape, 0) + 1).astype(F32)
    mag = jnp.exp(steps * (dt * a_re))
    ang = steps * (dt * a_im)
    pw_re = mag * jnp.cos(ang)
    pw_im = mag * jnp.sin(ang)
    pwre_ref[...] = pw_re
    pwim_ref[...] = pw_im
    ab_re, ab_im = pw_re[0:1], pw_im[0:1]
    den = a_re * a_re + a_im * a_im
    n_re = ab_re - 1.0
    co_re = (n_re * a_re + ab_im * a_im) / den
    co_im = (ab_im * a_re - n_re * a_im) / den
    b_re, b_im = bre_ref[...], bim_ref[...]
    bbre_ref[...] = co_re * b_re - co_im * b_im
    bbim_ref[...] = co_re * b_im + co_im * b_re


def _ssm_prep(log_dt, a_re, a_im, b_re, b_im):
    g, p = a_re.shape
    n = g * p
    flat = lambda t: t.reshape(1, n)
    cmaj = lambda t: jnp.transpose(t, (2, 0, 1)).reshape(SSM_GROUP, n)
    pw_re, pw_im, bb_re, bb_im = pl.pallas_call(
        _ssm_prep_kernel,
        out_shape=[jax.ShapeDtypeStruct((SUBLANES, n), F32)] * 2 + [jax.ShapeDtypeStruct((SSM_GROUP, n), F32)] * 2,
        name="ssm_prep",
    )(flat(a_re), flat(a_im), flat(jnp.repeat(log_dt, p)), cmaj(b_re), cmaj(b_im))
    eye = jnp.eye(g, dtype=F32)

    def expand(bb):
        return (bb.reshape(1, SSM_GROUP, g, p) * eye[:, None, :, None]).reshape(g * SSM_GROUP, n)

    b_big = jnp.concatenate([expand(bb_re), expand(bb_im)], axis=1).astype(BF16)
    return pw_re, pw_im, b_big


def _c_big(c_re, c_im):
    g, c, p = c_re.shape
    eye = jnp.eye(g, dtype=F32)

    def expand(t):
        return (jnp.transpose(t, (0, 2, 1))[:, :, None, :] * eye[:, None, :, None]).reshape(g * p, g * c)

    return jnp.concatenate([expand(c_re), -expand(c_im)], axis=0).astype(BF16)


def _ssm_kernel(u_ref, h0re_ref, h0im_ref, pwre_ref, pwim_ref, bbig_ref, cbig_ref, d_ref, wglu_ref, bglu_ref,
                out_ref, hre_ref, him_ref, bu_scr, h_scr, cre_scr, cim_scr, *, per_slab_state):
    i = pl.program_id(1)
    tt = u_ref.shape[1]
    n = pwre_ref.shape[1]
    u = u_ref[0]
    bu_scr[...] = jnp.dot(u.astype(BF16), bbig_ref[...], preferred_element_type=F32)

    if not per_slab_state:
        @pl.when(i == 0)
        def _():
            cre_scr[...] = h0re_ref[0]
            cim_scr[...] = h0im_ref[0]

    pw_re, pw_im = pwre_ref[...], pwim_ref[...]
    rows = lax.broadcasted_iota(jnp.int32, (SUBLANES, n), 0)
    step_coef = []
    for sh in (1, 2, 4):
        keep = rows >= sh
        step_coef.append((sh, jnp.where(keep, jnp.broadcast_to(pw_re[sh - 1:sh], (SUBLANES, n)), 0.0),
                          jnp.where(keep, jnp.broadcast_to(pw_im[sh - 1:sh], (SUBLANES, n)), 0.0)))

    def slab(j, carry):
        c_re, c_im = carry
        r0 = pl.multiple_of(j * SUBLANES, SUBLANES)
        if per_slab_state:
            c_re = h0re_ref[0, pl.ds(j, 1), :]
            c_im = h0im_ref[0, pl.ds(j, 1), :]
        x_re = bu_scr[pl.ds(r0, SUBLANES), 0:n]
        x_im = bu_scr[pl.ds(r0, SUBLANES), n:2 * n]
        for sh, a_re, a_im in step_coef:
            s_re = pltpu.roll(x_re, sh, 0)
            s_im = pltpu.roll(x_im, sh, 0)
            x_re, x_im = x_re + a_re * s_re - a_im * s_im, x_im + a_re * s_im + a_im * s_re
        x_re, x_im = x_re + pw_re * c_re - pw_im * c_im, x_im + pw_re * c_im + pw_im * c_re
        h_scr[pl.ds(r0, SUBLANES), 0:n] = x_re
        h_scr[pl.ds(r0, SUBLANES), n:2 * n] = x_im
        c_re, c_im = x_re[SUBLANES - 1:SUBLANES], x_im[SUBLANES - 1:SUBLANES]
        if per_slab_state:
            hre_ref[0, pl.ds(j, 1), :] = c_re
            him_ref[0, pl.ds(j, 1), :] = c_im
        return c_re, c_im

    if per_slab_state:
        init = (jnp.zeros((1, n), F32), jnp.zeros((1, n), F32))
    else:
        init = (cre_scr[...], cim_scr[...])
    c_re, c_im = lax.fori_loop(0, tt // SUBLANES, slab, init)
    if not per_slab_state:
        cre_scr[...] = c_re
        cim_scr[...] = c_im
        hre_ref[0] = c_re
        him_ref[0] = c_im

    y = jnp.dot(h_scr[...].astype(BF16), cbig_ref[...], preferred_element_type=F32) + d_ref[...] * u
    gl = jax.nn.gelu(y)
    gate = jnp.dot(gl.astype(BF16), wglu_ref[...], preferred_element_type=F32) + bglu_ref[...]
    out_ref[0] = gl * jax.nn.sigmoid(gate)


def _ssm(u, h0_re, h0_im, pw_re, pw_im, b_big, c_big, d_skip, w_glu_bf, b_glu, tile, per_slab_state):
    b, t, d_ssm = u.shape
    ns, n = h0_re.shape[1], h0_re.shape[2]
    st = pl.BlockSpec((1, ns, n), lambda bi, i: (bi, 0, 0))
    tok = pl.BlockSpec((1, tile, d_ssm), lambda bi, i: (bi, i, 0))
    return pl.pallas_call(
        functools.partial(_ssm_kernel, per_slab_state=per_slab_state),
        grid=(b, t // tile),
        in_specs=[tok, st, st, _const_spec((SUBLANES, n)), _const_spec((SUBLANES, n)),
                  _const_spec((d_ssm, 2 * n)), _const_spec((2 * n, d_ssm)), _const_spec((1, d_ssm)),
                  _const_spec((d_ssm, d_ssm)), _const_spec((1, d_ssm))],
        out_specs=[tok, st, st],
        out_shape=[jax.ShapeDtypeStruct((b, t, d_ssm), F32), jax.ShapeDtypeStruct((b, ns, n), F32),
                   jax.ShapeDtypeStruct((b, ns, n), F32)],
        scratch_shapes=[pltpu.VMEM((tile, 2 * n), F32), pltpu.VMEM((tile, 2 * n), F32),
                        pltpu.VMEM((1, n), F32), pltpu.VMEM((1, n), F32)],
        compiler_params=_cparams("arbitrary", "arbitrary"),
        name="ssm_slab" if per_slab_state else "ssm_seq",
    )(u, h0_re, h0_im, pw_re, pw_im, b_big, c_big, d_skip, w_glu_bf, b_glu)


def _pool_kernel(halo_ref, u_ref, w_ref, scale_ref, o_ref, *, start_pos, zero_first_halo):
    i = pl.program_id(1)
    tt, c = u_ref.shape[1], u_ref.shape[2]
    halo = halo_ref[0]
    if zero_first_halo:
        halo = jnp.where(i > 0, halo, 0.0)
    u = u_ref[0]
    xe = jnp.concatenate([halo, u], axis=0)
    sums, acc, sh = [], xe, 1
    for _ in POOL_WINDOWS:
        acc = acc + pltpu.roll(acc, sh, 0)
        sums.append(acc[POOL_HALO:])
        sh *= 2
    lane = lax.broadcasted_iota(jnp.int32, (tt, c), 1)
    group_width = c // len(POOL_WINDOWS)
    win, wsz = sums[-1], jnp.full((tt, c), POOL_WINDOWS[-1], jnp.int32)
    for gi in range(len(POOL_WINDOWS) - 2, -1, -1):
        in_group = lane < (gi + 1) * group_width
        win = jnp.where(in_group, sums[gi], win)
        wsz = jnp.where(in_group, POOL_WINDOWS[gi], wsz)
    pos = start_pos + i * tt + lax.broadcasted_iota(jnp.int32, (tt, c), 0)
    cnt = jnp.minimum(pos + 1, wsz).astype(F32)
    pooled = win / cnt - u
    o_ref[0] = jnp.dot(pooled.astype(BF16), w_ref[...], preferred_element_type=F32) * scale_ref[...]


def _pool(u, prefix16, w_big_bf, scale, tile, start_pos):
    b, t, c = u.shape
    tok = pl.BlockSpec((1, tile, c), lambda bi, i: (bi, i, 0))
    if prefix16 is None:
        per = tile // POOL_HALO
        halo_src = u
        halo = pl.BlockSpec((1, POOL_HALO, c), lambda bi, i: (bi, jnp.maximum(i * per - 1, 0), 0))
    else:
        assert t == tile
        halo_src = prefix16
        halo = pl.BlockSpec((1, POOL_HALO, c), lambda bi, i: (bi, 0, 0))
    return pl.pallas_call(
        functools.partial(_pool_kernel, start_pos=start_pos, zero_first_halo=prefix16 is None),
        grid=(b, t // tile),
        in_specs=[halo, tok, _const_spec((c, c)), _const_spec((1, c))],
        out_specs=tok,
        out_shape=jax.ShapeDtypeStruct((b, t, c), F32),
        compiler_params=_cparams("parallel", "arbitrary"),
        name="pool_seq" if prefix16 is None else "pool_prefix",
    )(halo_src, u, w_big_bf, scale)


def _ffn(x1, g2_ref, wup_ref, cw_ref, cb_ref, wdown_ref, acc_scr, shifted_rows):
    hn2 = _rms(x1, g2_ref[...]).astype(BF16)
    acc_scr[...] = jnp.zeros_like(acc_scr)
    n_chunks = wup_ref.shape[1]

    def chunk(c, carry):
        act = []
        for part in range(2):
            up = jnp.dot(hn2, wup_ref[part, c], preferred_element_type=F32)
            up1, up2 = shifted_rows(up, part, c)
            cw = cw_ref[part, c]
            act.append(cb_ref[part, c] + up2 * cw[0:1] + up1 * cw[1:2] + up * cw[2:3])
        h = (jax.nn.silu(act[1]) * act[0]).astype(BF16)
        acc_scr[...] += jnp.dot(h, wdown_ref[c], preferred_element_type=F32)
        return carry

    lax.fori_loop(0, n_chunks, chunk, 0)
    return x1 + acc_scr[...]


def _mix_out(att_ref, ssm_ref, pool_ref, x_ref, gatt_ref, gssm_ref, gpool_ref, wout_ref):
    d_ssm = ssm_ref.shape[-1]
    mix = jnp.dot(_rms(att_ref[0], gatt_ref[...]).astype(BF16), wout_ref[0:D_ATT, :], preferred_element_type=F32)
    mix += jnp.dot(_rms(ssm_ref[0], gssm_ref[...]).astype(BF16), wout_ref[D_ATT:D_ATT + d_ssm, :],
                   preferred_element_type=F32)
    mix += jnp.dot(_rms(pool_ref[0], gpool_ref[...]).astype(BF16), wout_ref[D_ATT + d_ssm:, :],
                   preferred_element_type=F32)
    return x_ref[0] + mix


def _out_seq_kernel(att_ref, ssm_ref, pool_ref, x_ref,
                    gatt_ref, gssm_ref, gpool_ref, wout_ref, g2_ref, wup_ref, cw_ref, cb_ref, wdown_ref, gf_ref,
                    *rest, final):
    if final:
        xo_ref, y_ref, conv_ref, acc_scr, tail_scr = rest
    else:
        xo_ref, conv_ref, acc_scr, tail_scr = rest
    i = pl.program_id(1)
    tt = x_ref.shape[1]

    @pl.when(i == 0)
    def _():
        tail_scr[...] = jnp.zeros_like(tail_scr)

    x1 = _mix_out(att_ref, ssm_ref, pool_ref, x_ref, gatt_ref, gssm_ref, gpool_ref, wout_ref)

    def shifted_rows(up, part, c):
        ext = jnp.concatenate([tail_scr[part, c], up], axis=0)
        tail_scr[part, c] = up[tt - SUBLANES:]
        conv_ref[0, part, c] = up[tt - (CONV_WIDTH - 1):]
        return pltpu.roll(ext, 1, 0)[SUBLANES:], pltpu.roll(ext, 2, 0)[SUBLANES:]

    x2 = _ffn(x1, g2_ref, wup_ref, cw_ref, cb_ref, wdown_ref, acc_scr, shifted_rows)
    xo_ref[0] = x2
    if final:
        y_ref[0] = _rms(x2, gf_ref[...])


def _out_slab_kernel(att_ref, ssm_ref, pool_ref, x_ref, p1_ref, p2_ref,
                     gatt_ref, gssm_ref, gpool_ref, wout_ref, g2_ref, wup_ref, cw_ref, cb_ref, wdown_ref, gf_ref,
                     *rest, final):
    if final:
        xo_ref, y_ref, up_ref, acc_scr = rest
    else:
        xo_ref, up_ref, acc_scr = rest
    tt = x_ref.shape[1]
    x1 = _mix_out(att_ref, ssm_ref, pool_ref, x_ref, gatt_ref, gssm_ref, gpool_ref, wout_ref)
    t_in = lax.broadcasted_iota(jnp.int32, (tt, FF_CHUNK), 0) & (SUBLANES - 1)

    def shifted_rows(up, part, c):
        up_ref[part, c] = up
        return (jnp.where(t_in >= 1, pltpu.roll(up, 1, 0), p1_ref[part, c]),
                jnp.where(t_in >= 2, pltpu.roll(up, 2, 0), p2_ref[part, c]))

    x2 = _ffn(x1, g2_ref, wup_ref, cw_ref, cb_ref, wdown_ref, acc_scr, shifted_rows)
    xo_ref[0] = x2
    if final:
        y_ref[0] = _rms(x2, gf_ref[...])


def _weight_specs(dm, d_ssm, d_pool, n_chunks):
    return [_const_spec((1, D_ATT)), _const_spec((1, d_ssm)), _const_spec((1, d_pool)),
            _const_spec((dm, dm)), _const_spec((1, dm)),
            _const_spec((2, n_chunks, dm, FF_CHUNK)), _const_spec((2, n_chunks, CONV_WIDTH, FF_CHUNK)),
            _const_spec((2, n_chunks, 1, FF_CHUNK)), _const_spec((n_chunks, FF_CHUNK, dm)), _const_spec((1, dm))]


def _out_seq(att, ssm_out, pool_out, x, lw, tile, final):
    b, t, dm = x.shape
    d_ssm, d_pool = ssm_out.shape[-1], pool_out.shape[-1]
    n_chunks = lw['w_down'].shape[0]
    tok = lambda c: pl.BlockSpec((1, tile, c), lambda bi, i: (bi, i, 0))
    conv_spec = pl.BlockSpec((1, 2, n_chunks, CONV_WIDTH - 1, FF_CHUNK), lambda bi, i: (bi, 0, 0, 0, 0))
    out_specs = [tok(dm)] + ([tok(dm)] if final else []) + [conv_spec]
    out_shape = ([jax.ShapeDtypeStruct((b, t, dm), F32)] * (2 if final else 1)
                 + [jax.ShapeDtypeStruct((b, 2, n_chunks, CONV_WIDTH - 1, FF_CHUNK), F32)])
    res = pl.pallas_call(
        functools.partial(_out_seq_kernel, final=final),
        grid=(b, t // tile),
        in_specs=[tok(D_ATT), tok(d_ssm), tok(d_pool), tok(dm)] + _weight_specs(dm, d_ssm, d_pool, n_chunks),
        out_specs=out_specs,
        out_shape=out_shape,
        scratch_shapes=[pltpu.VMEM((tile, dm), F32), pltpu.VMEM((2, n_chunks, SUBLANES, FF_CHUNK), F32)],
        compiler_params=_cparams("parallel", "arbitrary"),
        name="out_seq",
    )(att, ssm_out, pool_out, x,
      lw['g_att'], lw['g_ssm'], lw['g_pool'], lw['w_out'], lw['g2'], lw['w_up'], lw['conv_w'], lw['conv_b'],
      lw['w_down'], lw['g_f'])
    conv_state = jnp.transpose(res[-1], (0, 3, 1, 2, 4)).reshape(b, CONV_WIDTH - 1, 2 * n_chunks * FF_CHUNK)
    return res[0], (res[1] if final else None), conv_state


def _out_slab(att, ssm_out, pool_out, x, conv_prefix, lw, final):
    _, r, dm = x.shape
    d_ssm, d_pool = ssm_out.shape[-1], pool_out.shape[-1]
    n_chunks = lw['w_down'].shape[0]
    b = conv_prefix.shape[0]
    t = r // b
    assert t == SUBLANES and CONV_WIDTH == 3

    def chunked(rows):
        return jnp.transpose(rows.reshape(r, 2, n_chunks, FF_CHUNK), (1, 2, 0, 3))

    zeros = jnp.zeros((b, 1, conv_prefix.shape[-1]), F32)
    p1 = jnp.concatenate([conv_prefix[:, 1:2]] + [zeros] * (t - 1), axis=1)
    p2 = jnp.concatenate([conv_prefix[:, 0:1], conv_prefix[:, 1:2]] + [zeros] * (t - 2), axis=1)
    p1, p2 = chunked(p1.reshape(r, -1)), chunked(p2.reshape(r, -1))
    full = lambda shape: pl.BlockSpec(shape, lambda: (0,) * len(shape))
    up_shape = (2, n_chunks, r, FF_CHUNK)
    out_specs = [full((1, r, dm))] + ([full((1, r, dm))] if final else []) + [full(up_shape)]
    out_shape = ([jax.ShapeDtypeStruct((1, r, dm), F32)] * (2 if final else 1)
                 + [jax.ShapeDtypeStruct(up_shape, F32)])
    wspecs = [full(s.block_shape) for s in _weight_specs(dm, d_ssm, d_pool, n_chunks)]
    res = pl.pallas_call(
        functools.partial(_out_slab_kernel, final=final),
        in_specs=[full((1, r, D_ATT)), full((1, r, d_ssm)), full((1, r, d_pool)), full((1, r, dm)),
                  full(up_shape), full(up_shape)] + wspecs,
        out_specs=out_specs,
        out_shape=out_shape,
        scratch_shapes=[pltpu.VMEM((r, dm), F32)],
        compiler_params=pltpu.CompilerParams(vmem_limit_bytes=VMEM_LIMIT),
        name="out_slab",
    )(att, ssm_out, pool_out, x, p1, p2,
      lw['g_att'], lw['g_ssm'], lw['g_pool'], lw['w_out'], lw['g2'], lw['w_up'], lw['conv_w'], lw['conv_b'],
      lw['w_down'], lw['g_f'])
    up = jnp.transpose(res[-1], (2, 0, 1, 3)).reshape(b, t, 2 * n_chunks * FF_CHUNK)
    return res[0], (res[1] if final else None), up[:, t - (CONV_WIDTH - 1):]


def _block_diag(w):
    g, a, b = w.shape
    eye = jnp.eye(g, dtype=w.dtype)
    return (w[:, :, None, :] * eye[:, None, :, None]).reshape(g * a, g * b)


def _layer_weights(i, p):
    dm = p['w_in'].shape[1]
    d_ff = p['w_down'].shape[1]
    n_chunks = d_ff // FF_CHUNK
    assert n_chunks * FF_CHUNK == d_ff
    row = lambda t: t.reshape(1, -1)
    chunk_cols = lambda t: jnp.transpose(t.reshape(t.shape[0], 2, n_chunks, FF_CHUNK), (1, 2, 0, 3))
    return {
        'g1': row(p['norm1_g'][i]), 'w_in': p['w_in'][i].astype(BF16),
        'g_att': row(p['out_norm_att'][i]), 'g_ssm': row(p['out_norm_ssm'][i]),
        'g_pool': row(p['out_norm_pool'][i]),
        'w_out': p['w_out'][i].astype(BF16), 'g2': row(p['norm2_g'][i]),
        'w_up': chunk_cols(p['w_up'][i].astype(BF16)),
        'conv_w': chunk_cols(p['conv_w'][i]), 'conv_b': chunk_cols(row(p['conv_b'][i])),
        'w_down': p['w_down'][i].astype(BF16).reshape(n_chunks, FF_CHUNK, dm),
        'g_f': row(p['norm_f_g']),
        'ssm_d': row(p['ssm_d'][i]), 'w_glu': p['ssm_w_glu'][i].astype(BF16), 'b_glu': row(p['ssm_b_glu'][i]),
        'pool_w': _block_diag(p['pool_w'][i]).astype(BF16), 'pool_scale': row(p['pool_scale'][i]),
    }


def kernel(x_prompt, x_sample, cache_k, cache_v, state_ssm_re, state_ssm_im, state_pool, state_conv, norm1_g, w_in, ssm_log_dt, ssm_a_re, ssm_a_im, ssm_b_re, ssm_b_im, ssm_c_re, ssm_c_im, ssm_d, ssm_w_glu, ssm_b_glu, pool_w, pool_scale, out_norm_att, out_norm_ssm, out_norm_pool, w_out, norm2_g, w_up, conv_w, conv_b, w_down, norm_f_g):
    params = dict(norm1_g=norm1_g, w_in=w_in, ssm_d=ssm_d, ssm_w_glu=ssm_w_glu, ssm_b_glu=ssm_b_glu,
                  pool_w=pool_w, pool_scale=pool_scale, out_norm_att=out_norm_att, out_norm_ssm=out_norm_ssm,
                  out_norm_pool=out_norm_pool, w_out=w_out, norm2_g=norm2_g, w_up=w_up, conv_w=conv_w,
                  conv_b=conv_b, w_down=w_down, norm_f_g=norm_f_g)
    depth = w_in.shape[0]
    bp, sp, dm = x_prompt.shape
    bs, ts, _ = x_sample.shape
    n_past = cache_k.shape[2]
    grp = ssm_a_re.shape[1]
    n_state = grp * ssm_a_re.shape[2]
    d_ssm, d_pool = ssm_d.shape[1], pool_scale.shape[1]
    pool_buf = state_pool.shape[2]
    n_keep = min(n_past, sp)
    assert ts == SUBLANES

    cache_kt = jnp.transpose(cache_k.astype(F32), (0, 1, 3, 4, 2))
    cache_vt = jnp.transpose(cache_v.astype(F32), (0, 1, 3, 4, 2))

    xp = x_prompt.astype(F32)
    xs = x_sample.astype(F32).reshape(1, bs * ts, dm)
    yp = ys = None
    st_p, st_s = [], []
    for i in range(depth):
        lw = _layer_weights(i, params)
        final = i == depth - 1
        pw_re, pw_im, b_big = _ssm_prep(ssm_log_dt[i], ssm_a_re[i], ssm_a_im[i], ssm_b_re[i], ssm_b_im[i])
        c_big = _c_big(ssm_c_re[i], ssm_c_im[i])

        q, k, v, u_ssm, u_pool = _in_proj(xp.reshape(bp * sp, dm), lw['g1'], lw['w_in'], d_ssm, d_pool, 512)
        seq = lambda t: t.reshape(bp, sp, t.shape[-1])
        pair_seq = lambda t: t.reshape(HEAD_PAIRS, bp, sp, LANES)
        att = _attn_seq(pair_seq(q), pair_seq(k), pair_seq(v))
        zero_h = jnp.zeros((bp, 1, n_state), F32)
        ssm_out, h_re, h_im = _ssm(seq(u_ssm), zero_h, zero_h, pw_re, pw_im, b_big, c_big, lw['ssm_d'],
                                   lw['w_glu'], lw['b_glu'], 512, False)
        pool_out = _pool(seq(u_pool), None, lw['pool_w'], lw['pool_scale'], 512, 0)
        xp, yp_i, conv_p = _out_seq(att, ssm_out, pool_out, xp, lw, 512, final)
        yp = yp_i if final else yp
        st_p.append((_pairs_to_heads(pair_seq(k)[:, :, sp - n_keep:]), _pairs_to_heads(pair_seq(v)[:, :, sp - n_keep:]),
                     h_re.reshape(bp, grp, -1), h_im.reshape(bp, grp, -1),
                     seq(u_pool)[:, sp - pool_buf:], conv_p))

        q, k, v, u_ssm, u_pool = _in_proj(xs[0], lw['g1'], lw['w_in'], d_ssm, d_pool, bs * ts)
        per_seq = lambda t: t.reshape(bs, ts, t.shape[-1])
        att = _attn_decode(q, k, v, cache_kt, cache_vt, i, ts)
        ssm_out, h_re, h_im = _ssm(u_ssm[None], state_ssm_re[i].reshape(1, bs, n_state),
                                   state_ssm_im[i].reshape(1, bs, n_state), pw_re, pw_im, b_big, c_big,
                                   lw['ssm_d'], lw['w_glu'], lw['b_glu'], bs * ts, True)
        prefix16 = jnp.concatenate([jnp.zeros((bs, POOL_HALO - pool_buf, d_pool), F32),
                                    state_pool[i].astype(F32)], axis=1)
        pool_out = _pool(per_seq(u_pool), prefix16, lw['pool_w'], lw['pool_scale'], ts, PAST_LEN)
        xs, ys_i, conv_s = _out_slab(att[None], ssm_out, pool_out.reshape(1, bs * ts, d_pool),
                                     xs, state_conv[i].astype(F32), lw, final)
        ys = ys_i if final else ys
        pool_state = jnp.concatenate([state_pool[i].astype(F32), per_seq(u_pool)], axis=1)[:, -pool_buf:]
        new_heads = lambda t: _pairs_to_heads(t.reshape(HEAD_PAIRS, bs, ts, LANES))
        st_s.append((new_heads(k), new_heads(v), h_re.reshape(bs, grp, -1), h_im.reshape(bs, grp, -1),
                     pool_state, conv_s))

    stack = lambda sts, j: jnp.stack([s[j] for s in sts], 0)
    return ((yp.astype(x_prompt.dtype), ys.reshape(bs, ts, dm).astype(x_sample.dtype))
            + tuple(stack(st_p, j) for j in range(6)) + tuple(stack(st_s, j) for j in range(6)))
```

```python
import functools
import math

import jax
import jax.numpy as jnp
import numpy as np
from jax import lax
from jax.experimental import pallas as pl
from jax.experimental.pallas import tpu as pltpu

F32 = jnp.float32
BF16 = jnp.bfloat16

N_HEADS = 8
HEAD_DIM = 64
D_ATT = N_HEADS * HEAD_DIM
DILATED_GROUPS = ((128, 1), (512, 4), (2048, 16))
ATT_BLOCK = 128
ATT_SPAN = ATT_BLOCK * max(d for _, d in DILATED_GROUPS)
SSM_GROUP = 16
SSM_STATE = 64
POOL_WINDOWS = (2, 4, 8, 16)
POOL_HALO = 16
CONV_WIDTH = 3
PAST_LEN = 16384
EPS = 1e-6
NEG = -1e30
Q_SCALE = math.log2(math.e) * HEAD_DIM ** -0.5

SUBLANES = 8
LANES = 128
HEAD_PAIRS = D_ATT // LANES
FF_CHUNK = 256
VMEM_LIMIT = 56 * 1024 * 1024


def _cparams(*sem):
    return pltpu.CompilerParams(dimension_semantics=sem, vmem_limit_bytes=VMEM_LIMIT)


def _rms(x, g):
    return x * lax.rsqrt(jnp.mean(x * x, axis=-1, keepdims=True) + EPS) * g


def _const_spec(shape):
    nd = len(shape)
    return pl.BlockSpec(shape, lambda *_: (0,) * nd, pipeline_mode=pl.Buffered(1))


def _in_proj_kernel(x_ref, g_ref, w_ref, q_ref, k_ref, v_ref, us_ref, up_ref):
    hn = _rms(x_ref[...], g_ref[...]).astype(BF16)
    z = jnp.dot(hn, w_ref[...], preferred_element_type=F32)
    d_ssm = us_ref.shape[-1]
    for p in range(HEAD_PAIRS):
        lo = p * LANES
        q_ref[p] = z[:, lo:lo + LANES] * Q_SCALE
        k_ref[p] = z[:, D_ATT + lo:D_ATT + lo + LANES]
        v_ref[p] = z[:, 2 * D_ATT + lo:2 * D_ATT + lo + LANES]
    us_ref[...] = z[:, 3 * D_ATT:3 * D_ATT + d_ssm]
    up_ref[...] = z[:, 3 * D_ATT + d_ssm:]


def _in_proj(x2d, g, w_bf, d_ssm, d_pool, tile):
    n, dm = x2d.shape
    d_in = w_bf.shape[1]
    row = lambda c: pl.BlockSpec((tile, c), lambda i: (i, 0))
    pairs = pl.BlockSpec((HEAD_PAIRS, tile, LANES), lambda i: (0, i, 0))
    return pl.pallas_call(
        _in_proj_kernel,
        grid=(n // tile,),
        in_specs=[row(dm), _const_spec((1, dm)), _const_spec((dm, d_in))],
        out_specs=[pairs] * 3 + [row(d_ssm), row(d_pool)],
        out_shape=[jax.ShapeDtypeStruct((HEAD_PAIRS, n, LANES), F32)] * 3
                  + [jax.ShapeDtypeStruct((n, d_ssm), F32), jax.ShapeDtypeStruct((n, d_pool), F32)],
        compiler_params=_cparams("parallel"),
        name="in_proj",
    )(x2d, g, w_bf)


def _pairs_to_heads(t):
    t = jnp.moveaxis(t, 0, -2)
    return t.reshape(t.shape[:-2] + (N_HEADS, HEAD_DIM))


def _softmax_pv(qm, k, v, bias):
    s = lax.dot_general(qm, k, (((1,), (1,)), ((), ())), preferred_element_type=F32) + bias
    m = jnp.max(s, axis=-1, keepdims=True)
    p = jnp.exp2(s - m)
    l = jnp.sum(p, axis=-1, keepdims=True)
    return jnp.dot(p.astype(BF16), v, preferred_element_type=F32), m, l


def _attn_seq_kernel(q_ref, kp_ref, kc_ref, vp_ref, vc_ref, o_ref, kwin, vwin, bias_scr, og_scr, m_scr, l_scr):
    n = pl.program_id(2)
    blk, span = ATT_BLOCK, ATT_SPAN
    kwin[0:span, :] = kp_ref[...]
    kwin[span:, :] = kc_ref[...]
    vwin[0:span, :] = vp_ref[...]
    vwin[span:, :] = vc_ref[...]
    row = lax.broadcasted_iota(jnp.int32, (blk, 2 * blk), 0)
    col = lax.broadcasted_iota(jnp.int32, (blk, 2 * blk), 1)
    rel = row + blk - col
    band = jnp.where(rel >= 0, jnp.where(rel <= blk, 0.0, NEG), NEG)
    bias_scr[0] = band
    bias_scr[1] = jnp.where(col >= blk, band, NEG)
    low_half = lax.broadcasted_iota(jnp.int32, (blk, LANES), 1) < HEAD_DIM
    zero = jnp.zeros((blk, LANES), BF16)

    for gi, (_, d) in enumerate(DILATED_GROUPS):
        shift = d.bit_length() - 1

        def unit(u, carry, gi=gi, d=d, shift=shift):
            res = u & (d - 1)
            sub = u >> shift
            q0 = sub * (blk * d) + res
            k0 = span + q0 - blk * d
            if d == 1:
                q0, k0 = pl.multiple_of(q0, blk), pl.multiple_of(k0, blk)
            qu = q_ref[pl.ds(q0, blk, stride=d), :].astype(BF16)
            ku = kwin[pl.ds(k0, 2 * blk, stride=d), :].astype(BF16)
            vu = vwin[pl.ds(k0, 2 * blk, stride=d), :].astype(BF16)
            first = jnp.logical_and(n == 0, sub == 0)
            bias = bias_scr[jnp.where(first, 1, 0)]
            pv_a, m_a, l_a = _softmax_pv(jnp.where(low_half, qu, zero), ku, vu, bias)
            pv_b, m_b, l_b = _softmax_pv(jnp.where(low_half, zero, qu), ku, vu, bias)
            rows = pl.ds(q0, blk, stride=d)
            og_scr[gi, rows, :] = jnp.where(low_half, pv_a, pv_b)
            m_scr[gi, rows, :] = jnp.where(low_half, m_a, m_b)
            l_scr[gi, rows, :] = jnp.where(low_half, l_a, l_b)
            return carry

        lax.fori_loop(0, span // blk, unit, 0, unroll=8)

    m1, m2, m3 = m_scr[0], m_scr[1], m_scr[2]
    mmax = jnp.maximum(jnp.maximum(m1, m2), m3)
    w1, w2, w3 = jnp.exp2(m1 - mmax), jnp.exp2(m2 - mmax), jnp.exp2(m3 - mmax)
    num = w1 * og_scr[0] + w2 * og_scr[1] + w3 * og_scr[2]
    o_ref[0] = num / (w1 * l_scr[0] + w2 * l_scr[1] + w3 * l_scr[2])


def _attn_seq(q, k, v):
    _, b, s, _ = q.shape
    span = ATT_SPAN
    assert s % span == 0 and len(DILATED_GROUPS) == 3
    cur = pl.BlockSpec((None, None, span, LANES), lambda bi, p, n: (p, bi, n, 0))
    prev = pl.BlockSpec((None, None, span, LANES), lambda bi, p, n: (p, bi, jnp.maximum(n - 1, 0), 0))
    return pl.pallas_call(
        _attn_seq_kernel,
        grid=(b, HEAD_PAIRS, s // span),
        in_specs=[cur, prev, cur, prev, cur],
        out_specs=pl.BlockSpec((1, span, LANES), lambda bi, p, n: (bi, n, p)),
        out_shape=jax.ShapeDtypeStruct((b, s, D_ATT), F32),
        scratch_shapes=[pltpu.VMEM((2 * span, LANES), F32), pltpu.VMEM((2 * span, LANES), F32),
                        pltpu.VMEM((2, ATT_BLOCK, 2 * ATT_BLOCK), F32)]
                       + [pltpu.VMEM((len(DILATED_GROUPS), span, LANES), F32)] * 3,
        compiler_params=_cparams("parallel", "parallel", "arbitrary"),
        name="attn_seq",
    )(q, k, k, v, v)


def _decode_multiplicity(n_past, t_new):
    j = np.arange(t_new)[:, None]
    k = np.arange(n_past + LANES)[None, :]
    delta = n_past + j - k
    c = np.zeros((t_new, n_past + LANES), np.float32)
    for w, d in DILATED_GROUPS:
        c += ((delta >= 0) & (delta <= w) & (delta % d == 0) & (k < n_past + t_new)).astype(np.float32)
    return c[:, :n_past], c[:, n_past:]


def _attn_decode_kernel(q_ref, kn_ref, vn_ref, kt_ref, vt_ref, cold_ref, cnew_ref, o_ref):
    t_new = q_ref.shape[1]
    c_old, c_new = cold_ref[...], cnew_ref[...]
    zpad = jnp.zeros((LANES - t_new, LANES), F32)
    contract_last = (((1,), (1,)), ((), ()))
    for p in range(HEAD_PAIRS):
        qp = q_ref[p]
        knp = jnp.concatenate([kn_ref[p], zpad], axis=0)
        vnp = jnp.concatenate([vn_ref[p], zpad], axis=0)
        outs = []
        for half in range(2):
            h = 2 * p + half
            sl = slice(half * HEAD_DIM, (half + 1) * HEAD_DIM)
            qh = qp[:, sl].astype(BF16)
            s_old = jnp.dot(qh, kt_ref[0, 0, h].astype(BF16), preferred_element_type=F32)
            s_new = lax.dot_general(qh, knp[:, sl].astype(BF16), contract_last, preferred_element_type=F32)
            s_old = jnp.where(c_old > 0, s_old, NEG)
            s_new = jnp.where(c_new > 0, s_new, NEG)
            m = jnp.maximum(jnp.max(s_old, axis=-1, keepdims=True), jnp.max(s_new, axis=-1, keepdims=True))
            p_old = c_old * jnp.exp2(s_old - m)
            p_new = c_new * jnp.exp2(s_new - m)
            l = jnp.sum(p_old, axis=-1, keepdims=True) + jnp.sum(p_new, axis=-1, keepdims=True)
            o = lax.dot_general(p_old.astype(BF16), vt_ref[0, 0, h].astype(BF16), contract_last,
                                preferred_element_type=F32)
            o += jnp.dot(p_new.astype(BF16), vnp[:, sl].astype(BF16), preferred_element_type=F32)
            outs.append(o / l)
        o_ref[:, p * LANES:(p + 1) * LANES] = jnp.concatenate(outs, axis=1)


def _attn_decode(q, k_new, v_new, cache_kt, cache_vt, layer, t_new):
    _, rows, _ = q.shape
    _, b, h, e, n_past = cache_kt.shape
    c_old, c_new = _decode_multiplicity(n_past, t_new)
    new = pl.BlockSpec((HEAD_PAIRS, t_new, LANES), lambda i: (0, i, 0))
    past = pl.BlockSpec((1, 1, h, e, n_past), lambda i: (layer, i, 0, 0, 0))
    return pl.pallas_call(
        _attn_decode_kernel,
        grid=(b,),
        in_specs=[new, new, new, past, past, _const_spec((t_new, n_past)), _const_spec((t_new, LANES))],
        out_specs=pl.BlockSpec((t_new, D_ATT), lambda i: (i, 0)),
        out_shape=jax.ShapeDtypeStruct((rows, D_ATT), F32),
        compiler_params=_cparams("parallel"),
        name="attn_decode",
    )(q, k_new, v_new, cache_kt, cache_vt, jnp.asarray(c_old), jnp.asarray(c_new))


def _ssm_prep_kernel(are_ref, aim_ref, ldt_ref, bre_ref, bim_ref, pwre_ref, pwim_ref, bbre_ref, bbim_ref):
    a_re, a_im = are_ref[...], aim_ref[...]
    dt = jnp.exp(ldt_ref[...])
    steps = (lax.broadcasted_iota(jnp.int32, pwre_ref.shape, 0) + 1).astype(F32)
    mag = jnp.exp(steps * (dt * a_re))
    ang = steps * (dt * a_im)
    pw_re = mag * jnp.cos(ang)
    pw_im = mag * jnp.sin(ang)
    pwre_ref[...] = pw_re
    pwim_ref[...] = pw_im
    ab_re, ab_im = pw_re[0:1], pw_im[0:1]
    den = a_re * a_re + a_im * a_im
    n_re = ab_re - 1.0
    co_re = (n_re * a_re + ab_im * a_im) / den
    co_im = (ab_im * a_re - n_re * a_im) / den
    b_re, b_im = bre_ref[...], bim_ref[...]
    bbre_ref[...] = co_re * b_re - co_im * b_im
    bbim_ref[...] = co_re * b_im + co_im * b_re


def _ssm_prep(log_dt, a_re, a_im, b_re, b_im):
    g, p = a_re.shape
    n = g * p
    flat = lambda t: t.reshape(1, n)
    cmaj = lambda t: jnp.transpose(t, (2, 0, 1)).reshape(SSM_GROUP, n)
    pw_re, pw_im, bb_re, bb_im = pl.pallas_call(
        _ssm_prep_kernel,
        out_shape=[jax.ShapeDtypeStruct((SUBLANES, n), F32)] * 2 + [jax.ShapeDtypeStruct((SSM_GROUP, n), F32)] * 2,
        name="ssm_prep",
    )(flat(a_re), flat(a_im), flat(jnp.repeat(log_dt, p)), cmaj(b_re), cmaj(b_im))
    eye = jnp.eye(g, dtype=F32)

    def expand(bb):
        return (bb.reshape(1, SSM_GROUP, g, p) * eye[:, None, :, None]).reshape(g * SSM_GROUP, n)

    b_big = jnp.concatenate([expand(bb_re), expand(bb_im)], axis=1).astype(BF16)
    return pw_re, pw_im, b_big


def _c_big(c_re, c_im):
    g, c, p = c_re.shape
    eye = jnp.eye(g, dtype=F32)

    def expand(t):
        return (jnp.transpose(t, (0, 2, 1))[:, :, None, :] * eye[:, None, :, None]).reshape(g * p, g * c)

    return jnp.concatenate([expand(c_re), -expand(c_im)], axis=0).astype(BF16)


def _ssm_kernel(u_ref, h0re_ref, h0im_ref, pwre_ref, pwim_ref, bbig_ref, cbig_ref, d_ref, wglu_ref, bglu_ref,
                out_ref, hre_ref, him_ref, bu_scr, h_scr, cre_scr, cim_scr, *, per_slab_state):
    i = pl.program_id(1)
    tt = u_ref.shape[1]
    n = pwre_ref.shape[1]
    u = u_ref[0]
    bu_scr[...] = jnp.dot(u.astype(BF16), bbig_ref[...], preferred_element_type=F32)

    if not per_slab_state:
        @pl.when(i == 0)
        def _():
            cre_scr[...] = h0re_ref[0]
            cim_scr[...] = h0im_ref[0]

    pw_re, pw_im = pwre_ref[...], pwim_ref[...]
    rows = lax.broadcasted_iota(jnp.int32, (SUBLANES, n), 0)
    step_coef = []
    for sh in (1, 2, 4):
        keep = rows >= sh
        step_coef.append((sh, jnp.where(keep, jnp.broadcast_to(pw_re[sh - 1:sh], (SUBLANES, n)), 0.0),
                          jnp.where(keep, jnp.broadcast_to(pw_im[sh - 1:sh], (SUBLANES, n)), 0.0)))

    def slab(j, carry):
        c_re, c_im = carry
        r0 = pl.multiple_of(j * SUBLANES, SUBLANES)
        if per_slab_state:
            c_re = h0re_ref[0, pl.ds(j, 1), :]
            c_im = h0im_ref[0, pl.ds(j, 1), :]
        x_re = bu_scr[pl.ds(r0, SUBLANES), 0:n]
        x_im = bu_scr[pl.ds(r0, SUBLANES), n:2 * n]
        for sh, a_re, a_im in step_coef:
            s_re = pltpu.roll(x_re, sh, 0)
            s_im = pltpu.roll(x_im, sh, 0)
            x_re, x_im = x_re + a_re * s_re - a_im * s_im, x_im + a_re * s_im + a_im * s_re
        x_re, x_im = x_re + pw_re * c_re - pw_im * c_im, x_im + pw_re * c_im + pw_im * c_re
        h_scr[pl.ds(r0, SUBLANES), 0:n] = x_re
        h_scr[pl.ds(r0, SUBLANES), n:2 * n] = x_im
        c_re, c_im = x_re[SUBLANES - 1:SUBLANES], x_im[SUBLANES - 1:SUBLANES]
        if per_slab_state:
            hre_ref[0, pl.ds(j, 1), :] = c_re
            him_ref[0, pl.ds(j, 1), :] = c_im
        return c_re, c_im

    if per_slab_state:
        init = (jnp.zeros((1, n), F32), jnp.zeros((1, n), F32))
    else:
        init = (cre_scr[...], cim_scr[...])
    c_re, c_im = lax.fori_loop(0, tt // SUBLANES, slab, init)
    if not per_slab_state:
        cre_scr[...] = c_re
        cim_scr[...] = c_im
        hre_ref[0] = c_re
        him_ref[0] = c_im

    y = jnp.dot(h_scr[...].astype(BF16), cbig_ref[...], preferred_element_type=F32) + d_ref[...] * u
    gl = jax.nn.gelu(y)
    gate = jnp.dot(gl.astype(BF16), wglu_ref[...], preferred_element_type=F32) + bglu_ref[...]
    out_ref[0] = gl * jax.nn.sigmoid(gate)


def _ssm(u, h0_re, h0_im, pw_re, pw_im, b_big, c_big, d_skip, w_glu_bf, b_glu, tile, per_slab_state):
    b, t, d_ssm = u.shape
    ns, n = h0_re.shape[1], h0_re.shape[2]
    st = pl.BlockSpec((1, ns, n), lambda bi, i: (bi, 0, 0))
    tok = pl.BlockSpec((1, tile, d_ssm), lambda bi, i: (bi, i, 0))
    return pl.pallas_call(
        functools.partial(_ssm_kernel, per_slab_state=per_slab_state),
        grid=(b, t // tile),
        in_specs=[tok, st, st, _const_spec((SUBLANES, n)), _const_spec((SUBLANES, n)),
                  _const_spec((d_ssm, 2 * n)), _const_spec((2 * n, d_ssm)), _const_spec((1, d_ssm)),
                  _const_spec((d_ssm, d_ssm)), _const_spec((1, d_ssm))],
        out_specs=[tok, st, st],
        out_shape=[jax.ShapeDtypeStruct((b, t, d_ssm), F32), jax.ShapeDtypeStruct((b, ns, n), F32),
                   jax.ShapeDtypeStruct((b, ns, n), F32)],
        scratch_shapes=[pltpu.VMEM((tile, 2 * n), F32), pltpu.VMEM((tile, 2 * n), F32),
                        pltpu.VMEM((1, n), F32), pltpu.VMEM((1, n), F32)],
        compiler_params=_cparams("arbitrary", "arbitrary"),
        name="ssm_slab" if per_slab_state else "ssm_seq",
    )(u, h0_re, h0_im, pw_re, pw_im, b_big, c_big, d_skip, w_glu_bf, b_glu)


def _pool_kernel(halo_ref, u_ref, w_ref, scale_ref, o_ref, *, start_pos, zero_first_halo):
    i = pl.program_id(1)
    tt, c = u_ref.shape[1], u_ref.shape[2]
    halo = halo_ref[0]
    if zero_first_halo:
        halo = jnp.where(i > 0, halo, 0.0)
    u = u_ref[0]
    xe = jnp.concatenate([halo, u], axis=0)
    sums, acc, sh = [], xe, 1
    for _ in POOL_WINDOWS:
        acc = acc + pltpu.roll(acc, sh, 0)
        sums.append(acc[POOL_HALO:])
        sh *= 2
    lane = lax.broadcasted_iota(jnp.int32, (tt, c), 1)
    group_width = c // len(POOL_WINDOWS)
    win, wsz = sums[-1], jnp.full((tt, c), POOL_WINDOWS[-1], jnp.int32)
    for gi in range(len(POOL_WINDOWS) - 2, -1, -1):
        in_group = lane < (gi + 1) * group_width
        win = jnp.where(in_group, sums[gi], win)
        wsz = jnp.where(in_group, POOL_WINDOWS[gi], wsz)
    pos = start_pos + i * tt + lax.broadcasted_iota(jnp.int32, (tt, c), 0)
    cnt = jnp.minimum(pos + 1, wsz).astype(F32)
    pooled = win / cnt - u
    o_ref[0] = jnp.dot(pooled.astype(BF16), w_ref[...], preferred_element_type=F32) * scale_ref[...]


def _pool(u, prefix16, w_big_bf, scale, tile, start_pos):
    b, t, c = u.shape
    tok = pl.BlockSpec((1, tile, c), lambda bi, i: (bi, i, 0))
    if prefix16 is None:
        per = tile // POOL_HALO
        halo_src = u
        halo = pl.BlockSpec((1, POOL_HALO, c), lambda bi, i: (bi, jnp.maximum(i * per - 1, 0), 0))
    else:
        assert t == tile
        halo_src = prefix16
        halo = pl.BlockSpec((1, POOL_HALO, c), lambda bi, i: (bi, 0, 0))
    return pl.pallas_call(
        functools.partial(_pool_kernel, start_pos=start_pos, zero_first_halo=prefix16 is None),
        grid=(b, t // tile),
        in_specs=[halo, tok, _const_spec((c, c)), _const_spec((1, c))],
        out_specs=tok,
        out_shape=jax.ShapeDtypeStruct((b, t, c), F32),
        compiler_params=_cparams("parallel", "arbitrary"),
        name="pool_seq" if prefix16 is None else "pool_prefix",
    )(halo_src, u, w_big_bf, scale)


def _ffn(x1, g2_ref, wup_ref, cw_ref, cb_ref, wdown_ref, h_scr, shifted_rows):
    hn2 = _rms(x1, g2_ref[...]).astype(BF16)
    for c in range(wup_ref.shape[1]):
        act = []
        for part in range(2):
            up = jnp.dot(hn2, wup_ref[part, c], preferred_element_type=F32)
            up1, up2 = shifted_rows(up, part, c)
            cw = cw_ref[part, c]
            act.append(cb_ref[part, c] + up2 * cw[0:1] + up1 * cw[1:2] + up * cw[2:3])
        h_scr[:, c * FF_CHUNK:(c + 1) * FF_CHUNK] = (jax.nn.silu(act[1]) * act[0]).astype(BF16)
    return x1 + jnp.dot(h_scr[...], wdown_ref[...], preferred_element_type=F32)


def _mix_out(att_ref, ssm_ref, pool_ref, x_ref, gatt_ref, gssm_ref, gpool_ref, wout_ref):
    d_ssm = ssm_ref.shape[-1]
    mix = jnp.dot(_rms(att_ref[0], gatt_ref[...]).astype(BF16), wout_ref[0:D_ATT, :], preferred_element_type=F32)
    mix += jnp.dot(_rms(ssm_ref[0], gssm_ref[...]).astype(BF16), wout_ref[D_ATT:D_ATT + d_ssm, :],
                   preferred_element_type=F32)
    mix += jnp.dot(_rms(pool_ref[0], gpool_ref[...]).astype(BF16), wout_ref[D_ATT + d_ssm:, :],
                   preferred_element_type=F32)
    return x_ref[0] + mix


def _out_seq_kernel(att_ref, ssm_ref, pool_ref, x_ref,
                    gatt_ref, gssm_ref, gpool_ref, wout_ref, g2_ref, wup_ref, cw_ref, cb_ref, wdown_ref, gf_ref,
                    *rest, final):
    if final:
        xo_ref, y_ref, conv_ref, h_scr, tail_scr = rest
    else:
        xo_ref, conv_ref, h_scr, tail_scr = rest
    i = pl.program_id(1)
    tt = x_ref.shape[1]

    @pl.when(i == 0)
    def _():
        tail_scr[...] = jnp.zeros_like(tail_scr)

    x1 = _mix_out(att_ref, ssm_ref, pool_ref, x_ref, gatt_ref, gssm_ref, gpool_ref, wout_ref)

    def shifted_rows(up, part, c):
        ext = jnp.concatenate([tail_scr[part, c], up], axis=0)
        tail_scr[part, c] = up[tt - SUBLANES:]
        conv_ref[0, part, c] = up[tt - (CONV_WIDTH - 1):]
        return pltpu.roll(ext, 1, 0)[SUBLANES:], pltpu.roll(ext, 2, 0)[SUBLANES:]

    x2 = _ffn(x1, g2_ref, wup_ref, cw_ref, cb_ref, wdown_ref, h_scr, shifted_rows)
    xo_ref[0] = x2
    if final:
        y_ref[0] = _rms(x2, gf_ref[...])


def _out_slab_kernel(att_ref, ssm_ref, pool_ref, x_ref, p1_ref, p2_ref,
                     gatt_ref, gssm_ref, gpool_ref, wout_ref, g2_ref, wup_ref, cw_ref, cb_ref, wdown_ref, gf_ref,
                     *rest, final):
    if final:
        xo_ref, y_ref, up_ref, h_scr = rest
    else:
        xo_ref, up_ref, h_scr = rest
    tt = x_ref.shape[1]
    x1 = _mix_out(att_ref, ssm_ref, pool_ref, x_ref, gatt_ref, gssm_ref, gpool_ref, wout_ref)
    t_in = lax.broadcasted_iota(jnp.int32, (tt, FF_CHUNK), 0) & (SUBLANES - 1)

    def shifted_rows(up, part, c):
        up_ref[part, c] = up
        return (jnp.where(t_in >= 1, pltpu.roll(up, 1, 0), p1_ref[part, c]),
                jnp.where(t_in >= 2, pltpu.roll(up, 2, 0), p2_ref[part, c]))

    x2 = _ffn(x1, g2_ref, wup_ref, cw_ref, cb_ref, wdown_ref, h_scr, shifted_rows)
    xo_ref[0] = x2
    if final:
        y_ref[0] = _rms(x2, gf_ref[...])


def _weight_specs(dm, d_ssm, d_pool, n_chunks):
    return [_const_spec((1, D_ATT)), _const_spec((1, d_ssm)), _const_spec((1, d_pool)),
            _const_spec((dm, dm)), _const_spec((1, dm)),
            _const_spec((2, n_chunks, dm, FF_CHUNK)), _const_spec((2, n_chunks, CONV_WIDTH, FF_CHUNK)),
            _const_spec((2, n_chunks, 1, FF_CHUNK)), _const_spec((n_chunks * FF_CHUNK, dm)), _const_spec((1, dm))]


def _out_seq(att, ssm_out, pool_out, x, lw, tile, final):
    b, t, dm = x.shape
    d_ssm, d_pool = ssm_out.shape[-1], pool_out.shape[-1]
    n_chunks = lw['w_up'].shape[1]
    tok = lambda c: pl.BlockSpec((1, tile, c), lambda bi, i: (bi, i, 0))
    conv_spec = pl.BlockSpec((1, 2, n_chunks, CONV_WIDTH - 1, FF_CHUNK), lambda bi, i: (bi, 0, 0, 0, 0))
    out_specs = [tok(dm)] + ([tok(dm)] if final else []) + [conv_spec]
    out_shape = ([jax.ShapeDtypeStruct((b, t, dm), F32)] * (2 if final else 1)
                 + [jax.ShapeDtypeStruct((b, 2, n_chunks, CONV_WIDTH - 1, FF_CHUNK), F32)])
    res = pl.pallas_call(
        functools.partial(_out_seq_kernel, final=final),
        grid=(b, t // tile),
        in_specs=[tok(D_ATT), tok(d_ssm), tok(d_pool), tok(dm)] + _weight_specs(dm, d_ssm, d_pool, n_chunks),
        out_specs=out_specs,
        out_shape=out_shape,
        scratch_shapes=[pltpu.VMEM((tile, n_chunks * FF_CHUNK), BF16),
                        pltpu.VMEM((2, n_chunks, SUBLANES, FF_CHUNK), F32)],
        compiler_params=_cparams("parallel", "arbitrary"),
        name="out_seq",
    )(att, ssm_out, pool_out, x,
      lw['g_att'], lw['g_ssm'], lw['g_pool'], lw['w_out'], lw['g2'], lw['w_up'], lw['conv_w'], lw['conv_b'],
      lw['w_down'], lw['g_f'])
    conv_state = jnp.transpose(res[-1], (0, 3, 1, 2, 4)).reshape(b, CONV_WIDTH - 1, 2 * n_chunks * FF_CHUNK)
    return res[0], (res[1] if final else None), conv_state


def _out_slab(att, ssm_out, pool_out, x, conv_prefix, lw, final):
    _, r, dm = x.shape
    d_ssm, d_pool = ssm_out.shape[-1], pool_out.shape[-1]
    n_chunks = lw['w_up'].shape[1]
    b = conv_prefix.shape[0]
    t = r // b
    assert t == SUBLANES and CONV_WIDTH == 3

    def chunked(rows):
        return jnp.transpose(rows.reshape(r, 2, n_chunks, FF_CHUNK), (1, 2, 0, 3))

    zeros = jnp.zeros((b, 1, conv_prefix.shape[-1]), F32)
    p1 = jnp.concatenate([conv_prefix[:, 1:2]] + [zeros] * (t - 1), axis=1)
    p2 = jnp.concatenate([conv_prefix[:, 0:1], conv_prefix[:, 1:2]] + [zeros] * (t - 2), axis=1)
    p1, p2 = chunked(p1.reshape(r, -1)), chunked(p2.reshape(r, -1))
    full = lambda shape: pl.BlockSpec(shape, lambda: (0,) * len(shape))
    up_shape = (2, n_chunks, r, FF_CHUNK)
    out_specs = [full((1, r, dm))] + ([full((1, r, dm))] if final else []) + [full(up_shape)]
    out_shape = ([jax.ShapeDtypeStruct((1, r, dm), F32)] * (2 if final else 1)
                 + [jax.ShapeDtypeStruct(up_shape, F32)])
    wspecs = [full(s.block_shape) for s in _weight_specs(dm, d_ssm, d_pool, n_chunks)]
    res = pl.pallas_call(
        functools.partial(_out_slab_kernel, final=final),
        in_specs=[full((1, r, D_ATT)), full((1, r, d_ssm)), full((1, r, d_pool)), full((1, r, dm)),
                  full(up_shape), full(up_shape)] + wspecs,
        out_specs=out_specs,
        out_shape=out_shape,
        scratch_shapes=[pltpu.VMEM((r, n_chunks * FF_CHUNK), BF16)],
        compiler_params=pltpu.CompilerParams(vmem_limit_bytes=VMEM_LIMIT),
        name="out_slab",
    )(att, ssm_out, pool_out, x, p1, p2,
      lw['g_att'], lw['g_ssm'], lw['g_pool'], lw['w_out'], lw['g2'], lw['w_up'], lw['conv_w'], lw['conv_b'],
      lw['w_down'], lw['g_f'])
    up = jnp.transpose(res[-1], (2, 0, 1, 3)).reshape(b, t, 2 * n_chunks * FF_CHUNK)
    return res[0], (res[1] if final else None), up[:, t - (CONV_WIDTH - 1):]


def _block_diag(w):
    g, a, b = w.shape
    eye = jnp.eye(g, dtype=w.dtype)
    return (w[:, :, None, :] * eye[:, None, :, None]).reshape(g * a, g * b)


def _layer_weights(i, p):
    dm = p['w_in'].shape[1]
    d_ff = p['w_down'].shape[1]
    n_chunks = d_ff // FF_CHUNK
    assert n_chunks * FF_CHUNK == d_ff
    row = lambda t: t.reshape(1, -1)
    chunk_cols = lambda t: jnp.transpose(t.reshape(t.shape[0], 2, n_chunks, FF_CHUNK), (1, 2, 0, 3))
    return {
        'g1': row(p['norm1_g'][i]), 'w_in': p['w_in'][i].astype(BF16),
        'g_att': row(p['out_norm_att'][i]), 'g_ssm': row(p['out_norm_ssm'][i]),
        'g_pool': row(p['out_norm_pool'][i]),
        'w_out': p['w_out'][i].astype(BF16), 'g2': row(p['norm2_g'][i]),
        'w_up': chunk_cols(p['w_up'][i].astype(BF16)),
        'conv_w': chunk_cols(p['conv_w'][i]), 'conv_b': chunk_cols(row(p['conv_b'][i])),
        'w_down': p['w_down'][i].astype(BF16),
        'g_f': row(p['norm_f_g']),
        'ssm_d': row(p['ssm_d'][i]), 'w_glu': p['ssm_w_glu'][i].astype(BF16), 'b_glu': row(p['ssm_b_glu'][i]),
        'pool_w': _block_diag(p['pool_w'][i]).astype(BF16), 'pool_scale': row(p['pool_scale'][i]),
    }


def kernel(x_prompt, x_sample, cache_k, cache_v, state_ssm_re, state_ssm_im, state_pool, state_conv, norm1_g, w_in, ssm_log_dt, ssm_a_re, ssm_a_im, ssm_b_re, ssm_b_im, ssm_c_re, ssm_c_im, ssm_d, ssm_w_glu, ssm_b_glu, pool_w, pool_scale, out_norm_att, out_norm_ssm, out_norm_pool, w_out, norm2_g, w_up, conv_w, conv_b, w_down, norm_f_g):
    params = dict(norm1_g=norm1_g, w_in=w_in, ssm_d=ssm_d, ssm_w_glu=ssm_w_glu, ssm_b_glu=ssm_b_glu,
                  pool_w=pool_w, pool_scale=pool_scale, out_norm_att=out_norm_att, out_norm_ssm=out_norm_ssm,
                  out_norm_pool=out_norm_pool, w_out=w_out, norm2_g=norm2_g, w_up=w_up, conv_w=conv_w,
                  conv_b=conv_b, w_down=w_down, norm_f_g=norm_f_g)
    depth = w_in.shape[0]
    bp, sp, dm = x_prompt.shape
    bs, ts, _ = x_sample.shape
    n_past = cache_k.shape[2]
    grp = ssm_a_re.shape[1]
    n_state = grp * ssm_a_re.shape[2]
    d_ssm, d_pool = ssm_d.shape[1], pool_scale.shape[1]
    pool_buf = state_pool.shape[2]
    n_keep = min(n_past, sp)
    assert ts == SUBLANES

    cache_kt = jnp.transpose(cache_k.astype(F32), (0, 1, 3, 4, 2))
    cache_vt = jnp.transpose(cache_v.astype(F32), (0, 1, 3, 4, 2))

    xp = x_prompt.astype(F32)
    xs = x_sample.astype(F32).reshape(1, bs * ts, dm)
    yp = ys = None
    st_p, st_s = [], []
    for i in range(depth):
        lw = _layer_weights(i, params)
        final = i == depth - 1
        pw_re, pw_im, b_big = _ssm_prep(ssm_log_dt[i], ssm_a_re[i], ssm_a_im[i], ssm_b_re[i], ssm_b_im[i])
        c_big = _c_big(ssm_c_re[i], ssm_c_im[i])

        q, k, v, u_ssm, u_pool = _in_proj(xp.reshape(bp * sp, dm), lw['g1'], lw['w_in'], d_ssm, d_pool, 512)
        seq = lambda t: t.reshape(bp, sp, t.shape[-1])
        pair_seq = lambda t: t.reshape(HEAD_PAIRS, bp, sp, LANES)
        att = _attn_seq(pair_seq(q), pair_seq(k), pair_seq(v))
        zero_h = jnp.zeros((bp, 1, n_state), F32)
        ssm_out, h_re, h_im = _ssm(seq(u_ssm), zero_h, zero_h, pw_re, pw_im, b_big, c_big, lw['ssm_d'],
                                   lw['w_glu'], lw['b_glu'], 512, False)
        pool_out = _pool(seq(u_pool), None, lw['pool_w'], lw['pool_scale'], 512, 0)
        xp, yp_i, conv_p = _out_seq(att, ssm_out, pool_out, xp, lw, 512, final)
        yp = yp_i if final else yp
        st_p.append((_pairs_to_heads(pair_seq(k)[:, :, sp - n_keep:]), _pairs_to_heads(pair_seq(v)[:, :, sp - n_keep:]),
                     h_re.reshape(bp, grp, -1), h_im.reshape(bp, grp, -1),
                     seq(u_pool)[:, sp - pool_buf:], conv_p))

        q, k, v, u_ssm, u_pool = _in_proj(xs[0], lw['g1'], lw['w_in'], d_ssm, d_pool, bs * ts)
        per_seq = lambda t: t.reshape(bs, ts, t.shape[-1])
        att = _attn_decode(q, k, v, cache_kt, cache_vt, i, ts)
        ssm_out, h_re, h_im = _ssm(u_ssm[None], state_ssm_re[i].reshape(1, bs, n_state),
                                   state_ssm_im[i].reshape(1, bs, n_state), pw_re, pw_im, b_big, c_big,
                                   lw['ssm_d'], lw['w_glu'], lw['b_glu'], bs * ts, True)
        prefix16 = jnp.concatenate([jnp.zeros((bs, POOL_HALO - pool_buf, d_pool), F32),
                                    state_pool[i].astype(F32)], axis=1)
        pool_out = _pool(per_seq(u_pool), prefix16, lw['pool_w'], lw['pool_scale'], ts, PAST_LEN)
        xs, ys_i, conv_s = _out_slab(att[None], ssm_out, pool_out.reshape(1, bs * ts, d_pool),
                                     xs, state_conv[i].astype(F32), lw, final)
        ys = ys_i if final else ys
        pool_state = jnp.concatenate([state_pool[i].astype(F32), per_seq(u_pool)], axis=1)[:, -pool_buf:]
        new_heads = lambda t: _pairs_to_heads(t.reshape(HEAD_PAIRS, bs, ts, LANES))
        st_s.append((new_heads(k), new_heads(v), h_re.reshape(bs, grp, -1), h_im.reshape(bs, grp, -1),
                     pool_state, conv_s))

    stack = lambda sts, j: jnp.stack([s[j] for s in sts], 0)
    return ((yp.astype(x_prompt.dtype), ys.reshape(bs, ts, dm).astype(x_sample.dtype))
            + tuple(stack(st_p, j) for j in range(6)) + tuple(stack(st_s, j) for j in range(6)))
```
